```python
import math
import jax
import jax.numpy as jnp
from jax import lax
import numpy as np

D_MODEL = 4096
BATCH = 4
SEQ = 2048
DEPTH = 2
DEC_BATCH = 128
DEC_SEQ = 4
PAST_LEN = 16384
PAGE_SIZE = 128

N_META = 16
RMS_EPS = 1e-6
W_S5 = D_MODEL // 4
S5_GROUP_CH = 16
S5_GROUPS = W_S5 // S5_GROUP_CH
S5_STATE = 64
S5_DT_MIN = 1e-3
S5_DT_MAX = 1e-1
W_RWKV = D_MODEL // 2
RWKV_HEAD = 64
RWKV_HEADS = W_RWKV // RWKV_HEAD
RWKV_DECAY_RANK = D_MODEL // 32
RWKV_ICLR_RANK = D_MODEL // 32
RWKV_GATE_RANK = D_MODEL // 8
RWKV_LN_EPS = 64e-5
C_RWKV = 3 * W_RWKV + RWKV_DECAY_RANK + RWKV_ICLR_RANK + RWKV_GATE_RANK
RWKV_SPLITS = (W_RWKV, 2 * W_RWKV, 3 * W_RWKV, 3 * W_RWKV + RWKV_DECAY_RANK,
               3 * W_RWKV + RWKV_DECAY_RANK + RWKV_ICLR_RANK)
W_HGRN = D_MODEL // 4
HGRN_DK = 128
HGRN_HEADS = W_HGRN // HGRN_DK
HGRN_DV = W_HGRN // HGRN_HEADS
HGRN_CHUNK = 64
C_HGRN = 4 * W_HGRN
N_BRANCH = 3
C_GATE = N_BRANCH * D_MODEL
C_IN = W_S5 + C_RWKV + C_HGRN + C_GATE
IN_SPLITS = (W_S5, W_S5 + C_RWKV, W_S5 + C_RWKV + C_HGRN)
N_GROUPS = 4
E_PER_GROUP = 8
N_EXPERTS = N_GROUPS * E_PER_GROUP
TOP_K = 2
D_EXPERT = D_MODEL // 4
EXPERT_BLOCK = 128

kernel_name = 'hybrid_s5_rwkv7_hgrn2_hmoe_step'


def rms_norm(x, w):
    xf = x.astype(jnp.float32)
    y = xf * lax.rsqrt(jnp.mean(xf * xf, axis=-1, keepdims=True) + RMS_EPS) * w.astype(jnp.float32)
    return y.astype(x.dtype)


def _linear_combine(left, right):
    return (left[0] * right[0], right[0] * left[1] + right[1])


def s5_mix(u, lam_re, lam_im, log_dt, b_re, b_im, c_re, c_im, d_skip, w_glu, b_glu, x0_re, x0_im):
    f32 = jnp.float32
    bsz, L, _ = u.shape
    uh = u.astype(f32).reshape(bsz, L, S5_GROUPS, S5_GROUP_CH)
    lam = lax.complex(lam_re.astype(f32), lam_im.astype(f32))
    dt = jnp.exp(log_dt.astype(f32))[:, None]
    a_bar = jnp.exp(lam * dt)
    b_bar = ((a_bar - 1.0) / lam)[..., None] * lax.complex(b_re.astype(f32), b_im.astype(f32))
    c = lax.complex(c_re.astype(f32), c_im.astype(f32))
    bu = jnp.einsum('gph,blgh->blgp', b_bar, uh.astype(jnp.complex64))
    x0 = lax.complex(x0_re.astype(f32), x0_im.astype(f32))
    bu = bu.at[:, 0].add(a_bar * x0)
    a_seq = jnp.broadcast_to(a_bar, bu.shape)
    _, xs = lax.associative_scan(_linear_combine, (a_seq, bu), axis=1)
    y = jnp.real(jnp.einsum('ghp,blgp->blgh', c, xs)) + d_skip.astype(f32) * uh
    y = jax.nn.gelu(y.reshape(bsz, L, W_S5))
    y = y * jax.nn.sigmoid(y @ w_glu.astype(f32) + b_glu.astype(f32))
    x_last = xs[:, -1]
    return y.astype(u.dtype), jnp.real(x_last), jnp.imag(x_last)


def wkv7_scan(r, w, k, v, kk, a, S0):
    tm = lambda t: jnp.moveaxis(t, 1, 0)

    def step(S, inp):
        r_t, w_t, k_t, v_t, kk_t, a_t = inp
        sa = jnp.einsum('bhvk,bhk->bhv', S, -kk_t)
        S = (S * w_t[:, :, None, :] + sa[..., None] * (kk_t * a_t)[:, :, None, :]
             + v_t[..., None] * k_t[:, :, None, :])
        return S, jnp.einsum('bhvk,bhk->bhv', S, r_t)

    S, y = lax.scan(step, S0, (tm(r), tm(w), tm(k), tm(v), tm(kk), tm(a)))
    return jnp.moveaxis(y, 0, 1), S


def rwkv7_mix(z, prev, mu, w0, w_up, a0, a_up, g_up, k_k, k_a, r_k, ln_w, ln_b, S0):
    f32 = jnp.float32
    bsz, L, _ = z.shape
    shifted = jnp.concatenate([prev[:, None].astype(z.dtype), z[:, :-1]], axis=1)
    zm = z + (shifted - z) * mu
    r, k, v, xw, xa, xg = jnp.split(zm, RWKV_SPLITS, axis=-1)
    w_log = -jax.nn.softplus(-(w0 + jnp.tanh(xw) @ w_up).astype(f32)) - 0.5
    decay = jnp.exp(-jnp.exp(w_log))
    a = jax.nn.sigmoid((a0 + xa @ a_up).astype(f32))
    g = jax.nn.sigmoid(xg) @ g_up
    heads = lambda t: t.astype(f32).reshape(bsz, L, RWKV_HEADS, RWKV_HEAD)
    kk = heads(k * k_k)
    kk = kk / jnp.maximum(jnp.sqrt(jnp.sum(kk * kk, axis=-1, keepdims=True)), 1e-12)
    k = k.astype(f32) * (1.0 + (a - 1.0) * k_a.astype(f32))
    r_h, k_h, v_h = heads(r), heads(k), heads(v)
    y, S = wkv7_scan(r_h, heads(decay), k_h, v_h, kk, heads(a), S0.astype(f32))
    mean = jnp.mean(y, axis=-1, keepdims=True)
    var = jnp.mean(jnp.square(y - mean), axis=-1, keepdims=True)
    y = ((y - mean) * lax.rsqrt(var + RWKV_LN_EPS)).reshape(bsz, L, W_RWKV)
    y = y * ln_w.astype(f32) + ln_b.astype(f32)
    bonus = jnp.sum(r_h * k_h * r_k.astype(f32), axis=-1, keepdims=True) * v_h
    y = (y + bonus.reshape(bsz, L, W_RWKV)) * g.astype(f32)
    return y.astype(z.dtype), S, z[:, -1]


def hgrn2_chunk(S, q, k, log_f, v):
    L = q.shape[2]
    b = jnp.cumsum(log_f, axis=2)
    causal = jnp.tril(jnp.ones((L, L), dtype=bool))
    rel = jnp.where(causal[:, :, None], b[:, :, :, None, :] - b[:, :, None, :, :], -jnp.inf)
    att = jnp.einsum('bhtd,bhtsd,bhsd->bhts', q, jnp.exp(rel), k)
    o = jnp.einsum('bhts,bhsv->bhtv', att, v) + jnp.einsum('bhtd,bhdv->bhtv', q * jnp.exp(b), S)
    b_end = b[:, :, -1:]
    S = jnp.exp(b_end[:, :, 0])[..., None] * S + jnp.einsum('bhsd,bhsv->bhdv', k * jnp.exp(b_end - b), v)
    return S, o


def hgrn2_scan(q, k, log_f, v, S0):
    bsz, nh, L, _ = q.shape
    lead = L % HGRN_CHUNK
    n_chunks = L // HGRN_CHUNK
    S, outs = S0, []
    if lead:
        S, o = hgrn2_chunk(S, q[:, :, :lead], k[:, :, :lead], log_f[:, :, :lead], v[:, :, :lead])
        outs.append(o)
    if n_chunks:
        def chunks(t):
            return jnp.moveaxis(t[:, :, lead:].reshape(bsz, nh, n_chunks, HGRN_CHUNK, t.shape[-1]), 2, 0)
        S, o = lax.scan(lambda s, c: hgrn2_chunk(s, *c), S, (chunks(q), chunks(k), chunks(log_f), chunks(v)))
        outs.append(jnp.moveaxis(o, 0, 2).reshape(bsz, nh, n_chunks * HGRN_CHUNK, -1))
    return jnp.concatenate(outs, axis=2), S


def hgrn2_mix(z, lb, norm_w, S0):
    f32 = jnp.float32
    bsz, L, _ = z.shape
    q, f_logit, i, g = jnp.split(z, 4, axis=-1)
    fl = f_logit.astype(f32)
    lbf = lb.astype(f32)
    log_f = jnp.logaddexp(jnp.log(lbf), jnp.log1p(-lbf) + jax.nn.log_sigmoid(fl))
    k = (1.0 - lbf) * jax.nn.sigmoid(-fl)
    hd = lambda t, d: t.astype(f32).reshape(bsz, L, HGRN_HEADS, d).transpose(0, 2, 1, 3)
    o, S = hgrn2_scan(hd(jax.nn.silu(q), HGRN_DK), hd(k, HGRN_DK), hd(log_f, HGRN_DK),
                      hd(i, HGRN_DV), S0.astype(f32))
    o = o * lax.rsqrt(jnp.mean(o * o, axis=-1, keepdims=True) + RMS_EPS)
    o = o.transpose(0, 2, 1, 3).reshape(bsz, L, W_HGRN) * norm_w.astype(f32) * jax.nn.silu(g.astype(f32))
    return o.astype(z.dtype), S


def grouped_experts(t, expert, gate, w_gate, w_up, w_down):
    T, D = t.shape
    TK = T * TOP_K
    flat_e = expert.reshape(TK)
    order = jnp.argsort(flat_e)
    e_sorted = flat_e[order]
    counts = jnp.bincount(flat_e, length=N_EXPERTS)
    padded = (counts + EXPERT_BLOCK - 1) // EXPERT_BLOCK * EXPERT_BLOCK
    pad_end = jnp.cumsum(padded)
    pad_start = pad_end - padded
    start = jnp.cumsum(counts) - counts
    dest = pad_start[e_sorted] + jnp.arange(TK, dtype=jnp.int32) - start[e_sorted]
    n_blocks = -(-TK // EXPERT_BLOCK) + N_EXPERTS
    R = n_blocks * EXPERT_BLOCK
    row_tok = jnp.zeros((R,), jnp.int32).at[dest].set((order // TOP_K).astype(jnp.int32))
    row_w = jnp.zeros((R,), gate.dtype).at[dest].set(gate.reshape(TK)[order])
    blk_start = jnp.arange(n_blocks, dtype=pad_end.dtype) * EXPERT_BLOCK
    blk_expert = jnp.minimum(jnp.searchsorted(pad_end, blk_start, side='right'), N_EXPERTS - 1)
    xb = t[row_tok].reshape(n_blocks, EXPERT_BLOCK, D)

    def run(args):
        xe, e = args
        hdn = jax.nn.silu(xe @ w_gate[e]) * (xe @ w_up[e])
        return hdn @ w_down[e]

    yb = lax.map(run, (xb, blk_expert)).reshape(R, D)
    return jnp.zeros_like(t).at[row_tok].add(yb * row_w[:, None])


def moe_ffn(n, rg_w, rg_b, re_w, re_b, w_gate, w_up, w_down):
    shp = n.shape
    t = n.reshape(-1, shp[-1])
    T = t.shape[0]
    p_grp = jax.nn.softmax((t @ rg_w + rg_b).astype(jnp.float32), axis=-1)
    p_g, g_idx = lax.top_k(p_grp, 1)
    e_logits = (t @ re_w + re_b).astype(jnp.float32).reshape(T, N_GROUPS, E_PER_GROUP)
    e_sel = jnp.take_along_axis(e_logits, g_idx[:, :, None], axis=1)[:, 0]
    p_e, e_loc = lax.top_k(jax.nn.softmax(e_sel, axis=-1), TOP_K)
    gate = p_g * p_e / jnp.sum(p_e, axis=-1, keepdims=True)
    expert = g_idx * E_PER_GROUP + e_loc
    y = grouped_experts(t, expert, gate.astype(t.dtype), w_gate, w_up, w_down)
    return y.reshape(shp)


def trunk(x, s5_re0, s5_im0, shift0, wkv0, hgrn0, p):
    bsz, L, _ = x.shape
    sm = jax.nn.softmax(p['hgrn_lb'].astype(jnp.float32), axis=0)
    lower = jnp.maximum(jnp.cumsum(sm, axis=0) - sm[0], 0.0)
    h = x
    n_re, n_im, n_shift, n_wkv, n_hg = [], [], [], [], []
    for l in range(DEPTH):
        n = rms_norm(h, p['norm_mix_w'][l])
        z = n @ p['w_in'][l]
        z_s5, z_rw, z_hg, z_gate = jnp.split(z, IN_SPLITS, axis=-1)
        y_s5, s_re, s_im = s5_mix(z_s5, p['s5_lam_re'][l], p['s5_lam_im'][l], p['s5_log_dt'][l],
                                  p['s5_b_re'][l], p['s5_b_im'][l], p['s5_c_re'][l], p['s5_c_im'][l],
                                  p['s5_d'][l], p['s5_w_glu'][l], p['s5_b_glu'][l], s5_re0[l], s5_im0[l])
        y_rw, s_wkv, s_shift = rwkv7_mix(z_rw, shift0[l], p['rwkv_mu'][l], p['rwkv_w0'][l], p['rwkv_w_up'][l],
                                         p['rwkv_a0'][l], p['rwkv_a_up'][l], p['rwkv_g_up'][l],
                                         p['rwkv_k_k'][l], p['rwkv_k_a'][l], p['rwkv_r_k'][l],
                                         p['rwkv_ln_w'][l], p['rwkv_ln_b'][l], wkv0[l])
        y_hg, s_hg = hgrn2_mix(z_hg, lower[l], p['hgrn_norm_w'][l], hgrn0[l])
        gates = jax.nn.sigmoid(z_gate.reshape(bsz, L, N_BRANCH, D_MODEL))
        merged = (gates[:, :, 0] * (y_s5 @ p['p_s5'][l])
                  + gates[:, :, 1] * (y_rw @ p['p_rwkv'][l])
                  + gates[:, :, 2] * (y_hg @ p['p_hgrn'][l]))
        h = h + merged @ p['w_out'][l]
        h = h + moe_ffn(rms_norm(h, p['norm_ffn_w'][l]), p['router_g_w'][l], p['router_g_b'][l],
                        p['router_e_w'][l], p['router_e_b'][l], p['moe_w_gate'][l], p['moe_w_up'][l],
                        p['moe_w_down'][l])
        n_re.append(s_re)
        n_im.append(s_im)
        n_shift.append(s_shift)
        n_wkv.append(s_wkv)
        n_hg.append(s_hg)
    y = rms_norm(h, p['final_norm_w'])
    st = lambda lst: jnp.stack(lst, axis=0).astype(x.dtype)
    return y, st(n_re), st(n_im), st(n_shift), st(n_wkv), st(n_hg)


def setup_inputs(seed: int = 0) -> dict:
    key = jax.random.key(seed)
    ks = iter(jax.random.split(key, 64))
    f32 = jnp.float32
    L, D = DEPTH, D_MODEL
    G, P, H = S5_GROUPS, S5_STATE, S5_GROUP_CH

    def nrm(shape, scale):
        return scale * jax.random.normal(next(ks), shape, f32)

    def uni(shape, lo, hi):
        return jax.random.uniform(next(ks), shape, f32, lo, hi)

    n_idx = jnp.arange(P, dtype=f32)
    return {
        'x_prompt': nrm((BATCH, SEQ, D), 1.0),
        'x_sample': nrm((DEC_BATCH, DEC_SEQ, D), 1.0),
        'state_s5_re': nrm((L, DEC_BATCH, G, P), 0.3),
        'state_s5_im': nrm((L, DEC_BATCH, G, P), 0.3),
        'state_rwkv_shift': nrm((L, DEC_BATCH, C_RWKV), 1.0),
        'state_rwkv_wkv': nrm((L, DEC_BATCH, RWKV_HEADS, RWKV_HEAD, RWKV_HEAD), 0.3),
        'state_hgrn': nrm((L, DEC_BATCH, HGRN_HEADS, HGRN_DK, HGRN_DV), 0.3),
        'meta_tokens': nrm((N_META, D), 1.0),
        'norm_mix_w': 1.0 + nrm((L, D), 0.02),
        'w_in': nrm((L, D, C_IN), D ** -0.5),
        's5_lam_re': -0.5 * (1.0 + nrm((L, G, P), 0.02)),
        's5_lam_im': math.pi * n_idx + nrm((L, G, P), 0.02),
        's5_log_dt': uni((L, G), math.log(S5_DT_MIN), math.log(S5_DT_MAX)),
        's5_b_re': nrm((L, G, P, H), (2 * H) ** -0.5),
        's5_b_im': nrm((L, G, P, H), (2 * H) ** -0.5),
        's5_c_re': nrm((L, G, H, P), (2 * P) ** -0.5),
        's5_c_im': nrm((L, G, H, P), (2 * P) ** -0.5),
        's5_d': nrm((L, G, H), 1.0),
        's5_w_glu': nrm((L, W_S5, W_S5), W_S5 ** -0.5),
        's5_b_glu': nrm((L, W_S5), 0.01),
        'rwkv_mu': uni((L, C_RWKV), 0.0, 1.0),
        'rwkv_w0': uni((L, W_RWKV), -6.5, -1.5),
        'rwkv_w_up': nrm((L, RWKV_DECAY_RANK, W_RWKV), 0.5 * RWKV_DECAY_RANK ** -0.5),
        'rwkv_a0': nrm((L, W_RWKV), 0.1),
        'rwkv_a_up': nrm((L, RWKV_ICLR_RANK, W_RWKV), 0.5 * RWKV_ICLR_RANK ** -0.5),
        'rwkv_g_up': nrm((L, RWKV_GATE_RANK, W_RWKV), RWKV_GATE_RANK ** -0.5),
        'rwkv_k_k': 0.85 + nrm((L, W_RWKV), 0.02),
        'rwkv_k_a': 1.0 + nrm((L, W_RWKV), 0.02),
        'rwkv_r_k': nrm((L, RWKV_HEADS, RWKV_HEAD), 0.1),
        'rwkv_ln_w': 1.0 + nrm((L, W_RWKV), 0.02),
        'rwkv_ln_b': nrm((L, W_RWKV), 0.01),
        'hgrn_lb': nrm((L, W_HGRN), 0.5),
        'hgrn_norm_w': 1.0 + nrm((L, W_HGRN), 0.02),
        'p_s5': nrm((L, W_S5, D), W_S5 ** -0.5),
        'p_rwkv': nrm((L, W_RWKV, D), W_RWKV ** -0.5),
        'p_hgrn': nrm((L, W_HGRN, D), W_HGRN ** -0.5),
        'w_out': nrm((L, D, D), D ** -0.5),
        'norm_ffn_w': 1.0 + nrm((L, D), 0.02),
        'router_g_w': nrm((L, D, N_GROUPS), D ** -0.5),
        'router_g_b': nrm((L, N_GROUPS), 0.01),
        'router_e_w': nrm((L, D, N_EXPERTS), D ** -0.5),
        'router_e_b': nrm((L, N_EXPERTS), 0.01),
        'moe_w_gate': nrm((L, N_EXPERTS, D, D_EXPERT), D ** -0.5),
        'moe_w_up': nrm((L, N_EXPERTS, D, D_EXPERT), D ** -0.5),
        'moe_w_down': nrm((L, N_EXPERTS, D_EXPERT, D), D_EXPERT ** -0.5),
        'final_norm_w': 1.0 + nrm((D,), 0.02),
    }


def reference(x_prompt, x_sample, state_s5_re, state_s5_im, state_rwkv_shift, state_rwkv_wkv, state_hgrn,
              meta_tokens, norm_mix_w, w_in, s5_lam_re, s5_lam_im, s5_log_dt, s5_b_re, s5_b_im, s5_c_re,
              s5_c_im, s5_d, s5_w_glu, s5_b_glu, rwkv_mu, rwkv_w0, rwkv_w_up, rwkv_a0, rwkv_a_up, rwkv_g_up,
              rwkv_k_k, rwkv_k_a, rwkv_r_k, rwkv_ln_w, rwkv_ln_b, hgrn_lb, hgrn_norm_w, p_s5, p_rwkv, p_hgrn,
              w_out, norm_ffn_w, router_g_w, router_g_b, router_e_w, router_e_b, moe_w_gate, moe_w_up,
              moe_w_down, final_norm_w):
    params = dict(norm_mix_w=norm_mix_w, w_in=w_in, s5_lam_re=s5_lam_re, s5_lam_im=s5_lam_im,
                  s5_log_dt=s5_log_dt, s5_b_re=s5_b_re, s5_b_im=s5_b_im, s5_c_re=s5_c_re, s5_c_im=s5_c_im,
                  s5_d=s5_d, s5_w_glu=s5_w_glu, s5_b_glu=s5_b_glu, rwkv_mu=rwkv_mu, rwkv_w0=rwkv_w0,
                  rwkv_w_up=rwkv_w_up, rwkv_a0=rwkv_a0, rwkv_a_up=rwkv_a_up, rwkv_g_up=rwkv_g_up,
                  rwkv_k_k=rwkv_k_k, rwkv_k_a=rwkv_k_a, rwkv_r_k=rwkv_r_k, rwkv_ln_w=rwkv_ln_w,
                  rwkv_ln_b=rwkv_ln_b, hgrn_lb=hgrn_lb, hgrn_norm_w=hgrn_norm_w, p_s5=p_s5, p_rwkv=p_rwkv,
                  p_hgrn=p_hgrn, w_out=w_out, norm_ffn_w=norm_ffn_w, router_g_w=router_g_w,
                  router_g_b=router_g_b, router_e_w=router_e_w, router_e_b=router_e_b,
                  moe_w_gate=moe_w_gate, moe_w_up=moe_w_up, moe_w_down=moe_w_down, final_norm_w=final_norm_w)
    f32 = jnp.float32
    bp = x_prompt.shape[0]
    meta = jnp.broadcast_to(meta_tokens[None].astype(x_prompt.dtype), (bp, N_META, D_MODEL))
    xp = jnp.concatenate([meta, x_prompt], axis=1)
    yp, p_re, p_im, p_shift, p_wkv, p_hg = trunk(
        xp,
        jnp.zeros((DEPTH, bp, S5_GROUPS, S5_STATE), f32),
        jnp.zeros((DEPTH, bp, S5_GROUPS, S5_STATE), f32),
        jnp.zeros((DEPTH, bp, C_RWKV), x_prompt.dtype),
        jnp.zeros((DEPTH, bp, RWKV_HEADS, RWKV_HEAD, RWKV_HEAD), f32),
        jnp.zeros((DEPTH, bp, HGRN_HEADS, HGRN_DK, HGRN_DV), f32),
        params)
    ys, s_re, s_im, s_shift, s_wkv, s_hg = trunk(
        x_sample, state_s5_re, state_s5_im, state_rwkv_shift, state_rwkv_wkv, state_hgrn, params)
    return (yp[:, N_META:], ys, p_re, p_im, p_shift, p_wkv, p_hg, s_re, s_im, s_shift, s_wkv, s_hg)
```

```python
import functools
import math

import jax
import jax.numpy as jnp
from jax import lax
from jax.experimental import pallas as pl
from jax.experimental.pallas import tpu as pltpu

F32 = jnp.float32
BF16 = jnp.bfloat16
HI = lax.Precision.HIGHEST

LANES = 128
SUBLANES = 8
VMEM_LIMIT = 56 * 1024 * 1024
RMS_EPS = 1e-6
RWKV_LN_EPS = 64e-5
TOP_K = 2
S5_GROUPS_PER_BLOCK = 8
HGRN_CHUNK = 48
HGRN_SUB = 16
HGRN_SAMPLE_SEQS = 8
MOE_ROWS = 256
MOE_TN = 256


def _cparams(n_axes):
    return pltpu.CompilerParams(dimension_semantics=("arbitrary",) * n_axes,
                                vmem_limit_bytes=VMEM_LIMIT)


def _pick(n, cands):
    for c in cands:
        if n % c == 0:
            return c
    raise ValueError(f"no tile for {n} in {cands}")


def _mm_kernel(a_ref, w_ref, o_ref, wc_ref):
    m = pl.program_id(1)
    k = pl.program_id(2)

    @pl.when(m == 0)
    def _():
        wc_ref[k] = w_ref[...].astype(BF16)

    p = jnp.dot(a_ref[...], wc_ref[k], preferred_element_type=F32)

    @pl.when(k == 0)
    def _():
        o_ref[...] = p

    @pl.when(k > 0)
    def _():
        o_ref[...] += p


def mm(a, w):
    M, K = a.shape
    N = w.shape[1]
    tm = _pick(M, (1280, 1024, 512, 256, 128))
    tn = _pick(N, (1280, 1024, 512, 256, 128))
    tk = _pick(K, (1024, 512, 256, 128, K))
    nk = K // tk
    return pl.pallas_call(
        _mm_kernel,
        grid=(N // tn, M // tm, nk),
        in_specs=[pl.BlockSpec((tm, tk), lambda n, m, k: (m, k)),
                  pl.BlockSpec((tk, tn), lambda n, m, k: (jnp.where(m == 0, k, nk - 1), n))],
        out_specs=pl.BlockSpec((tm, tn), lambda n, m, k: (m, n)),
        out_shape=jax.ShapeDtypeStruct((M, N), F32),
        scratch_shapes=[pltpu.VMEM((nk, tk, tn), BF16)],
        compiler_params=_cparams(3),
    )(a, w)


def _mm_hi_kernel(a_ref, w_ref, o_ref):
    o_ref[...] = jnp.dot(a_ref[...], w_ref[...], precision=HI, preferred_element_type=F32)


def mm_hi(a, w):
    M, K = a.shape
    N = w.shape[1]
    tm = _pick(M, (256, 128))
    return pl.pallas_call(
        _mm_hi_kernel,
        grid=(M // tm,),
        in_specs=[pl.BlockSpec((tm, K), lambda m: (m, 0)),
                  pl.BlockSpec((K, N), lambda m: (0, 0))],
        out_specs=pl.BlockSpec((tm, N), lambda m: (m, 0)),
        out_shape=jax.ShapeDtypeStruct((M, N), F32),
        compiler_params=_cparams(1),
    )(a, w)


def _rms_kernel(x_ref, w_ref, o32_ref, o16_ref):
    x = x_ref[...]
    y = x * lax.rsqrt(jnp.mean(x * x, axis=-1, keepdims=True) + RMS_EPS) * w_ref[...]
    o32_ref[...] = y
    o16_ref[...] = y.astype(BF16)


def rms_norm(x, w):
    M, D = x.shape
    tm = _pick(M, (256, 128))
    return pl.pallas_call(
        _rms_kernel,
        grid=(M // tm,),
        in_specs=[pl.BlockSpec((tm, D), lambda m: (m, 0)),
                  pl.BlockSpec((1, D), lambda m: (0, 0))],
        out_specs=[pl.BlockSpec((tm, D), lambda m: (m, 0)),
                   pl.BlockSpec((tm, D), lambda m: (m, 0))],
        out_shape=[jax.ShapeDtypeStruct((M, D), F32), jax.ShapeDtypeStruct((M, D), BF16)],
        compiler_params=_cparams(1),
    )(x, w.reshape(1, D))


def _gelu_tanh(y):
    return 0.5 * y * (1.0 + jnp.tanh(math.sqrt(2.0 / math.pi) * (y + 0.044715 * (y * y * y))))


def _s5_kernel(u_ref, bm_ref, cm_ref, are_ref, aim_ref, d_ref, x0re_ref, x0im_ref,
               y_ref, xlre_ref, xlim_ref, xs_ref, sre_ref, sim_ref, *, steps, rows, half):
    tc = pl.program_id(2)

    @pl.when(tc == 0)
    def _():
        sre_ref[...] = x0re_ref[...]
        sim_ref[...] = x0im_ref[...]

    u = u_ref[...]
    xs_ref[...] = jnp.dot(u, bm_ref[0], precision=HI, preferred_element_type=F32)
    a_re = are_ref[0]
    a_im = aim_ref[0]

    def step(t, carry):
        x_re, x_im = carry
        r0 = pl.multiple_of(t * rows, rows)
        b_re = xs_ref[pl.ds(r0, rows), 0:half]
        b_im = xs_ref[pl.ds(r0, rows), half:2 * half]
        n_re = a_re * x_re - a_im * x_im + b_re
        n_im = a_re * x_im + a_im * x_re + b_im
        xs_ref[pl.ds(r0, rows), 0:half] = n_re
        xs_ref[pl.ds(r0, rows), half:2 * half] = n_im
        return n_re, n_im

    x_re, x_im = lax.fori_loop(0, steps, step, (sre_ref[...], sim_ref[...]))
    sre_ref[...] = x_re
    sim_ref[...] = x_im
    xlre_ref[...] = x_re
    xlim_ref[...] = x_im
    y = jnp.dot(xs_ref[...], cm_ref[0], precision=HI, preferred_element_type=F32) + d_ref[0] * u
    y_ref[...] = _gelu_tanh(y)


def s5_scan(u2, bm, cm, a_re, a_im, dvec, x0_re, x0_im, *, groups, ntc, steps, rows):
    n_rows, W = u2.shape
    gbn = W // LANES
    half = bm.shape[2] // 2
    blk = steps * rows
    kern = functools.partial(_s5_kernel, steps=steps, rows=rows, half=half)
    return pl.pallas_call(
        kern,
        grid=(gbn, groups, ntc),
        in_specs=[pl.BlockSpec((blk, LANES), lambda g, s, t: (s * ntc + t, g)),
                  pl.BlockSpec((1, LANES, 2 * half), lambda g, s, t: (g, 0, 0)),
                  pl.BlockSpec((1, 2 * half, LANES), lambda g, s, t: (g, 0, 0)),
                  pl.BlockSpec((1, 1, half), lambda g, s, t: (g, 0, 0)),
                  pl.BlockSpec((1, 1, half), lambda g, s, t: (g, 0, 0)),
                  pl.BlockSpec((1, 1, LANES), lambda g, s, t: (g, 0, 0)),
                  pl.BlockSpec((rows, half), lambda g, s, t: (s, g)),
                  pl.BlockSpec((rows, half), lambda g, s, t: (s, g))],
        out_specs=[pl.BlockSpec((blk, LANES), lambda g, s, t: (s * ntc + t, g)),
                   pl.BlockSpec((rows, half), lambda g, s, t: (s, g)),
                   pl.BlockSpec((rows, half), lambda g, s, t: (s, g))],
        out_shape=[jax.ShapeDtypeStruct((n_rows, W), F32),
                   jax.ShapeDtypeStruct(x0_re.shape, F32),
                   jax.ShapeDtypeStruct(x0_im.shape, F32)],
        scratch_shapes=[pltpu.VMEM((blk, 2 * half), F32),
                        pltpu.VMEM((rows, half), F32),
                        pltpu.VMEM((rows, half), F32)],
        compiler_params=_cparams(3),
    )(u2, bm, cm, a_re, a_im, dvec, x0_re, x0_im)


def _s5_params(lam_re, lam_im, log_dt, b_re, b_im, c_re, c_im, d_skip):
    G, P, H = b_re.shape
    gpb = S5_GROUPS_PER_BLOCK
    gbn = G // gpb
    lam = lax.complex(lam_re.astype(F32), lam_im.astype(F32))
    dt = jnp.exp(log_dt.astype(F32))[:, None]
    a_bar = jnp.exp(lam * dt)
    b_bar = ((a_bar - 1.0) / lam)[..., None] * lax.complex(b_re.astype(F32), b_im.astype(F32))
    eye = jnp.eye(gpb, dtype=F32)

    def bdiag_in(m):
        m = m.reshape(gbn, gpb, P, H).transpose(0, 1, 3, 2)
        return jnp.einsum('aghp,gk->aghkp', m, eye).reshape(gbn, gpb * H, gpb * P)

    def bdiag_out(m):
        m = m.reshape(gbn, gpb, H, P)
        return jnp.einsum('aghp,gk->agpkh', m, eye).reshape(gbn, gpb * P, gpb * H)

    bm = jnp.concatenate([bdiag_in(jnp.real(b_bar)), bdiag_in(jnp.imag(b_bar))], axis=2)
    cm = jnp.concatenate([bdiag_out(c_re.astype(F32)), bdiag_out(-c_im.astype(F32))], axis=1)
    a_re = jnp.real(a_bar).reshape(gbn, 1, gpb * P)
    a_im = jnp.imag(a_bar).reshape(gbn, 1, gpb * P)
    dvec = d_skip.astype(F32).reshape(gbn, 1, gpb * H)
    return bm, cm, a_re, a_im, dvec


def _wkv_kernel(kk_ref, w_ref, b_ref, k_ref, r_ref, v_ref, s0_ref, y_ref, so_ref, s_ref, *, steps, n):
    tc = pl.program_id(1)

    @pl.when(tc == 0)
    def _():
        s_ref[...] = s0_ref[...]

    def tstep(t, _):
        kk = kk_ref[t]
        w = w_ref[t]
        b = b_ref[t]
        kx = k_ref[t]
        r = r_ref[t]

        def vstep(v, _):
            sv = s_ref[v]
            sa = -jnp.sum(sv * kk, axis=0, keepdims=True)
            vv = v_ref[t, pl.ds(v, 1), :]
            sn = sv * w + sa * b + vv * kx
            s_ref[v] = sn
            y_ref[t, pl.ds(v, 1), :] = jnp.sum(sn * r, axis=0, keepdims=True)
            return 0

        lax.fori_loop(0, n, vstep, 0, unroll=4)
        return 0

    lax.fori_loop(0, steps, tstep, 0)
    so_ref[...] = s_ref[...]


def wkv7(kk, w, b, k, r, v, s0):
    NT, N, NH = kk.shape
    steps = _pick(NT, (48, 16, 8, 4, 2, 1))
    seq_spec = pl.BlockSpec((steps, N, LANES), lambda h, t: (t, 0, h))
    st_spec = pl.BlockSpec((N, N, LANES), lambda h, t: (0, 0, h))
    kern = functools.partial(_wkv_kernel, steps=steps, n=N)
    return pl.pallas_call(
        kern,
        grid=(NH // LANES, NT // steps),
        in_specs=[seq_spec] * 6 + [st_spec],
        out_specs=[seq_spec, st_spec],
        out_shape=[jax.ShapeDtypeStruct((NT, N, NH), F32), jax.ShapeDtypeStruct((N, N, NH), F32)],
        scratch_shapes=[pltpu.VMEM((N, N, LANES), F32)],
        compiler_params=_cparams(2),
    )(kk, w, b, k, r, v, s0)


def _hgrn_gates(zq, zf, loglb, log1mlb, onemlb):
    q = zq * jax.nn.sigmoid(zq)
    log_sig = jnp.minimum(zf, 0.0) - jnp.log1p(jnp.exp(-jnp.abs(zf)))
    t = log1mlb + log_sig
    hi = jnp.maximum(loglb, t)
    log_f = hi + jnp.log1p(jnp.exp(-jnp.abs(loglb - t)))
    k = onemlb * jax.nn.sigmoid(-zf)
    return q, log_f, k


def _hgrn_out(o, zg, nw):
    o = o * lax.rsqrt(jnp.mean(o * o, axis=-1, keepdims=True) + RMS_EPS)
    return o * nw * (zg * jax.nn.sigmoid(zg))


def _col_bcast(row, n_lanes):
    e = jnp.concatenate([row, jnp.zeros((SUBLANES - 1, row.shape[1]), F32)], axis=0)
    ones = jnp.concatenate([jnp.ones((1, n_lanes), F32), jnp.zeros((SUBLANES - 1, n_lanes), F32)], axis=0)
    return lax.dot_general(e, ones, (((0,), (0,)), ((), ())), precision=HI, preferred_element_type=F32)


def _hgrn_prompt_kernel(zq_ref, zf_ref, zi_ref, zg_ref, loglb_ref, log1mlb_ref, onemlb_ref, nw_ref,
                        o_ref, so_ref, s_ref, *, rows, sub):
    c = pl.program_id(2)

    @pl.when(c == 0)
    def _():
        s_ref[...] = jnp.zeros_like(s_ref)

    q, log_f, k = _hgrn_gates(zq_ref[...], zf_ref[...], loglb_ref[0], log1mlb_ref[0], onemlb_ref[0])
    v = zi_ref[...]
    dv = v.shape[1]
    ri = lax.broadcasted_iota(jnp.int32, (rows, rows), 0)
    ci = lax.broadcasted_iota(jnp.int32, (rows, rows), 1)
    tri = (ci <= ri).astype(F32)
    bcum = jnp.dot(tri, log_f, precision=HI, preferred_element_type=F32)
    s_old = s_ref[...]
    o_parts = []
    for i in range(rows // sub):
        lo, hi = i * sub, (i + 1) * sub
        b_i = bcum[lo:hi]
        q_i = q[lo:hi]
        o_i = jnp.dot(q_i * jnp.exp(b_i), s_old, precision=HI, preferred_element_type=F32)
        if i > 0:
            b_ref_row = bcum[lo - 1:lo]
            qe = q_i * jnp.exp(b_i - b_ref_row)
            ke = k[0:lo] * jnp.exp(b_ref_row - bcum[0:lo])
            att = lax.dot_general(qe, ke, (((1,), (1,)), ((), ())), precision=HI, preferred_element_type=F32)
            o_i = o_i + jnp.dot(att, v[0:lo], precision=HI, preferred_element_type=F32)
        t_idx = lax.broadcasted_iota(jnp.int32, (sub, 1), 0)
        for s in range(sub):
            e_s = jnp.exp(jnp.minimum(b_i - b_i[s:s + 1], 0.0))
            col = jnp.sum(q_i * k[lo + s:lo + s + 1] * e_s, axis=-1, keepdims=True)
            col = jnp.where(t_idx >= s, col, 0.0)
            o_i = o_i + col * v[lo + s:lo + s + 1]
        o_parts.append(o_i)
    o = jnp.concatenate(o_parts, axis=0)
    b_end = bcum[rows - 1:rows]
    ke_end = k * jnp.exp(b_end - bcum)
    s_new = (_col_bcast(jnp.exp(b_end), dv) * s_old
             + lax.dot_general(ke_end, v, (((0,), (0,)), ((), ())), precision=HI, preferred_element_type=F32))
    s_ref[...] = s_new
    so_ref[0, 0] = s_new
    o_ref[...] = _hgrn_out(o, zg_ref[...], nw_ref[0])


def _hgrn_sample_kernel(zq_ref, zf_ref, zi_ref, zg_ref, loglb_ref, log1mlb_ref, onemlb_ref, nw_ref, s0_ref,
                        o_ref, so_ref, *, nseq, slen):
    rows = nseq * slen
    q, log_f, k = _hgrn_gates(zq_ref[...], zf_ref[...], loglb_ref[0], log1mlb_ref[0], onemlb_ref[0])
    v = zi_ref[...]
    dv = v.shape[1]
    ri = lax.broadcasted_iota(jnp.int32, (rows, rows), 0)
    ci = lax.broadcasted_iota(jnp.int32, (rows, rows), 1)
    same = (ri // slen) == (ci // slen)
    tri = ((ci <= ri) & same).astype(F32)
    bcum = jnp.dot(tri, log_f, precision=HI, preferred_element_type=F32)
    t_idx = lax.broadcasted_iota(jnp.int32, (rows, 1), 0)
    o = jnp.zeros((rows, dv), F32)
    for s in range(rows):
        e_s = jnp.exp(jnp.minimum(bcum - bcum[s:s + 1], 0.0))
        col = jnp.sum(q * k[s:s + 1] * e_s, axis=-1, keepdims=True)
        col = jnp.where((t_idx >= s) & (t_idx < (s // slen + 1) * slen), col, 0.0)
        o = o + col * v[s:s + 1]
    qe = q * jnp.exp(bcum)
    for j in range(nseq):
        in_seq = (t_idx >= j * slen) & (t_idx < (j + 1) * slen)
        s_old = s0_ref[j, 0]
        o = o + jnp.dot(jnp.where(in_seq, qe, 0.0), s_old, precision=HI, preferred_element_type=F32)
        b_end = bcum[(j + 1) * slen - 1:(j + 1) * slen]
        ke = jnp.where(in_seq, k * jnp.exp(b_end - bcum), 0.0)
        so_ref[j, 0] = (_col_bcast(jnp.exp(b_end), dv) * s_old
                        + lax.dot_general(ke, v, (((0,), (0,)), ((), ())), precision=HI,
                                          preferred_element_type=F32))
    o_ref[...] = _hgrn_out(o, zg_ref[...], nw_ref[0])


def hgrn2(z, col0, lb, norm_w, s0_sample, *, bp, lp, bs, ls, heads, dk):
    assert dk == LANES and col0 % LANES == 0
    cb = col0 // LANES
    loglb = jnp.log(lb).reshape(heads, 1, dk)
    log1mlb = jnp.log1p(-lb).reshape(heads, 1, dk)
    onemlb = (1.0 - lb).reshape(heads, 1, dk)
    nw = norm_w.astype(F32).reshape(heads, 1, dk)
    W = heads * dk
    par_spec3 = pl.BlockSpec((1, 1, dk), lambda b, h, c: (h, 0, 0))

    rc = HGRN_CHUNK
    assert lp % rc == 0
    nch = lp // rc

    def zspec(sec):
        return pl.BlockSpec((rc, dk), lambda b, h, c: (b * nch + c, cb + sec * heads + h))

    o_p, s_p = pl.pallas_call(
        functools.partial(_hgrn_prompt_kernel, rows=rc, sub=HGRN_SUB),
        grid=(bp, heads, nch),
        in_specs=[zspec(0), zspec(1), zspec(2), zspec(3), par_spec3, par_spec3, par_spec3, par_spec3],
        out_specs=[pl.BlockSpec((rc, dk), lambda b, h, c: (b * nch + c, h)),
                   pl.BlockSpec((1, 1, dk, dk), lambda b, h, c: (b, h, 0, 0))],
        out_shape=[jax.ShapeDtypeStruct((bp * lp, W), F32),
                   jax.ShapeDtypeStruct((bp, heads, dk, dk), F32)],
        scratch_shapes=[pltpu.VMEM((dk, dk), F32)],
        compiler_params=_cparams(3),
    )(z, z, z, z, loglb, log1mlb, onemlb, nw)

    nseq = HGRN_SAMPLE_SEQS
    rs = nseq * ls
    assert bs % nseq == 0 and rs % SUBLANES == 0 and (bp * lp) % rs == 0
    r0 = bp * lp // rs
    par_spec2 = pl.BlockSpec((1, 1, dk), lambda p, h: (h, 0, 0))

    def zspec_s(sec):
        return pl.BlockSpec((rs, dk), lambda p, h: (r0 + p, cb + sec * heads + h))

    st_spec = pl.BlockSpec((nseq, 1, dk, dk), lambda p, h: (p, h, 0, 0))
    o_s, s_s = pl.pallas_call(
        functools.partial(_hgrn_sample_kernel, nseq=nseq, slen=ls),
        grid=(bs // nseq, heads),
        in_specs=[zspec_s(0), zspec_s(1), zspec_s(2), zspec_s(3), par_spec2, par_spec2, par_spec2, par_spec2,
                  st_spec],
        out_specs=[pl.BlockSpec((rs, dk), lambda p, h: (p, h)), st_spec],
        out_shape=[jax.ShapeDtypeStruct((bs * ls, W), F32),
                   jax.ShapeDtypeStruct((bs, heads, dk, dk), F32)],
        compiler_params=_cparams(2),
    )(z, z, z, z, loglb, log1mlb, onemlb, nw, s0_sample.astype(F32))
    return o_p, o_s, s_p, s_s


def _moe_up_kernel(be_ref, nb_ref, x_ref, wg_ref, wu_ref, h_ref, wgc_ref, wuc_ref):
    i = pl.program_id(1)
    valid = i < nb_ref[0]
    prev = be_ref[jnp.maximum(i - 1, 0)]
    fresh = valid & ((i == 0) | (be_ref[i] != prev))

    @pl.when(fresh)
    def _():
        wgc_ref[...] = wg_ref[0].astype(BF16)
        wuc_ref[...] = wu_ref[0].astype(BF16)

    @pl.when(valid)
    def _():
        x = x_ref[...]
        g = jnp.dot(x, wgc_ref[...], preferred_element_type=F32)
        u = jnp.dot(x, wuc_ref[...], preferred_element_type=F32)
        h_ref[...] = (g * jax.nn.sigmoid(g) * u).astype(BF16)

    @pl.when(jnp.logical_not(valid))
    def _():
        h_ref[...] = jnp.zeros_like(h_ref)


def _moe_down_kernel(be_ref, nb_ref, h_ref, wd_ref, y_ref, wdc_ref):
    i = pl.program_id(1)
    valid = i < nb_ref[0]
    prev = be_ref[jnp.maximum(i - 1, 0)]
    fresh = valid & ((i == 0) | (be_ref[i] != prev))

    @pl.when(fresh)
    def _():
        wdc_ref[...] = wd_ref[0].astype(BF16)

    @pl.when(valid)
    def _():
        y_ref[...] = jnp.dot(h_ref[...], wdc_ref[...], preferred_element_type=F32)

    @pl.when(jnp.logical_not(valid))
    def _():
        y_ref[...] = jnp.zeros_like(y_ref)


def moe_experts(xs, blk_expert, n_used, w_gate, w_up, w_down):
    R, D = xs.shape
    E, _, DE = w_gate.shape
    nblk = R // MOE_ROWS
    tn = _pick(DE, (MOE_TN, 128))
    hid = pl.pallas_call(
        _moe_up_kernel,
        grid_spec=pltpu.PrefetchScalarGridSpec(
            num_scalar_prefetch=2,
            grid=(DE // tn, nblk),
            in_specs=[pl.BlockSpec((MOE_ROWS, D), lambda j, i, be, nb: (jnp.minimum(i, nb[0] - 1), 0)),
                      pl.BlockSpec((1, D, tn), lambda j, i, be, nb: (be[i], 0, j)),
                      pl.BlockSpec((1, D, tn), lambda j, i, be, nb: (be[i], 0, j))],
            out_specs=pl.BlockSpec((MOE_ROWS, tn), lambda j, i, be, nb: (i, j)),
            scratch_shapes=[pltpu.VMEM((D, tn), BF16), pltpu.VMEM((D, tn), BF16)]),
        out_shape=jax.ShapeDtypeStruct((R, DE), BF16),
        compiler_params=_cparams(2),
    )(blk_expert, n_used, xs, w_gate, w_up)
    tn2 = _pick(D, (1024, 512, 256, 128))
    return pl.pallas_call(
        _moe_down_kernel,
        grid_spec=pltpu.PrefetchScalarGridSpec(
            num_scalar_prefetch=2,
            grid=(D // tn2, nblk),
            in_specs=[pl.BlockSpec((MOE_ROWS, DE), lambda j, i, be, nb: (jnp.minimum(i, nb[0] - 1), 0)),
                      pl.BlockSpec((1, DE, tn2), lambda j, i, be, nb: (be[i], 0, j))],
            out_specs=pl.BlockSpec((MOE_ROWS, tn2), lambda j, i, be, nb: (i, j)),
            scratch_shapes=[pltpu.VMEM((DE, tn2), BF16)]),
        out_shape=jax.ShapeDtypeStruct((R, D), F32),
        compiler_params=_cparams(2),
    )(blk_expert, n_used, hid, w_down)


def moe_ffn(n32, n16, t_real, rg_w, rg_b, re_w, re_b, w_gate, w_up, w_down):
    Tp, D = n32.shape
    n_grp = rg_w.shape[1]
    n_exp = re_w.shape[1]
    e_per = n_exp // n_grp
    wr = jnp.concatenate([rg_w, re_w], axis=1).astype(F32)
    wr = jnp.pad(wr, ((0, 0), (0, LANES - wr.shape[1])))
    logits = mm_hi(n32, wr)[:t_real]
    p_grp = jax.nn.softmax(logits[:, :n_grp] + rg_b.astype(F32), axis=-1)
    p_g, g_idx = lax.top_k(p_grp, 1)
    e_logits = (logits[:, n_grp:n_grp + n_exp] + re_b.astype(F32)).reshape(t_real, n_grp, e_per)
    e_sel = jnp.take_along_axis(e_logits, g_idx[:, :, None], axis=1)[:, 0]
    p_e, e_loc = lax.top_k(jax.nn.softmax(e_sel, axis=-1), TOP_K)
    gate = p_g * p_e / jnp.sum(p_e, axis=-1, keepdims=True)
    expert = g_idx * e_per + e_loc

    tk = t_real * TOP_K
    flat_e = expert.reshape(tk).astype(jnp.int32)
    order = jnp.argsort(flat_e)
    e_sorted = flat_e[order]
    counts = jnp.bincount(flat_e, length=n_exp)
    padded = (counts + MOE_ROWS - 1) // MOE_ROWS * MOE_ROWS
    pad_end = jnp.cumsum(padded)
    pad_start = pad_end - padded
    start = jnp.cumsum(counts) - counts
    dest = (pad_start[e_sorted] + jnp.arange(tk, dtype=jnp.int32) - start[e_sorted]).astype(jnp.int32)
    n_blocks = -(-tk // MOE_ROWS) + n_exp
    R = n_blocks * MOE_ROWS
    row_tok = jnp.zeros((R,), jnp.int32).at[dest].set((order // TOP_K).astype(jnp.int32))
    n_used = (pad_end[-1] // MOE_ROWS).astype(jnp.int32)
    blk_start = jnp.arange(n_blocks, dtype=jnp.int32) * MOE_ROWS
    blk_start = jnp.minimum(blk_start, (n_used - 1) * MOE_ROWS)
    blk_expert = jnp.minimum(jnp.searchsorted(pad_end, blk_start, side='right'), n_exp - 1).astype(jnp.int32)
    xs = n16[row_tok]
    yb = moe_experts(xs, blk_expert, n_used.reshape(1), w_gate, w_up, w_down)
    slot = jnp.zeros((tk,), jnp.int32).at[order].set(dest).reshape(t_real, TOP_K)
    y = yb[slot[:, 0]] * gate[:, 0:1] + yb[slot[:, 1]] * gate[:, 1:2]
    return jnp.pad(y, ((0, Tp - t_real), (0, 0)))


def _split_groups(x, bp, lp, bs, ls):
    n = x.shape[-1]
    xp = x[:bp * lp].reshape(bp, lp, n)
    xs = x[bp * lp:bp * lp + bs * ls].reshape(bs, ls, n)
    return xp, xs


def _s5_branch(u, prm, x0re_s, x0im_s, w_glu, b_glu, *, bp, lp, bs, ls, tp):
    W = u.shape[1]
    bm, cm, a_re, a_im, dvec = prm
    n_state = x0re_s.shape[1] * x0re_s.shape[2]
    up, us = _split_groups(u, bp, lp, bs, ls)
    rows_p = SUBLANES
    steps_p = _pick(lp, (344, 48, 16, 8))
    up_tm = jnp.pad(up, ((0, rows_p - bp), (0, 0), (0, 0))).transpose(1, 0, 2).reshape(lp * rows_p, W)
    zero = jnp.zeros((rows_p, n_state), F32)
    yp, pre, pim = s5_scan(up_tm, bm, cm, a_re, a_im, dvec, zero, zero,
                           groups=1, ntc=lp // steps_p, steps=steps_p, rows=rows_p)
    yp = yp.reshape(lp, rows_p, W).transpose(1, 0, 2)[:bp].reshape(bp * lp, W)
    rows_s = _pick(bs, (32, 16, 8))
    ng = bs // rows_s
    us_tm = us.reshape(ng, rows_s, ls, W).transpose(0, 2, 1, 3).reshape(bs * ls, W)
    ys, sre, sim = s5_scan(us_tm, bm, cm, a_re, a_im, dvec,
                           x0re_s.reshape(bs, n_state).astype(F32), x0im_s.reshape(bs, n_state).astype(F32),
                           groups=ng, ntc=1, steps=ls, rows=rows_s)
    ys = ys.reshape(ng, ls, rows_s, W).transpose(0, 2, 1, 3).reshape(bs * ls, W)
    y = jnp.concatenate([yp, ys, jnp.zeros((tp - bp * lp - bs * ls, W), F32)], axis=0)
    gl = mm(y.astype(BF16), w_glu)
    y = y * jax.nn.sigmoid(gl + b_glu.astype(F32))
    gshape = x0re_s.shape[1:]
    return (y, pre[:bp].reshape((bp,) + gshape), pim[:bp].reshape((bp,) + gshape),
            sre.reshape((bs,) + gshape), sim.reshape((bs,) + gshape))


def _heads_on_lanes(x, nb, nl, heads, n):
    x = x.reshape(nb, nl, heads, n).transpose(1, 3, 0, 2).reshape(nl, n, nb * heads)
    pad = (-(nb * heads)) % LANES
    return jnp.pad(x, ((0, 0), (0, 0), (0, pad))) if pad else x


def _rwkv_branch(zr, prev_s, wkv0_s, p, *, bp, lp, bs, ls, tp, heads, n):
    wr = heads * n
    rk_d = p['w_up'].shape[0]
    ra_d = p['a_up'].shape[0]
    zp, zs = _split_groups(zr, bp, lp, bs, ls)
    sh_p = jnp.concatenate([jnp.zeros((bp, 1, zr.shape[1]), F32), zp[:, :-1]], axis=1)
    sh_s = jnp.concatenate([prev_s[:, None].astype(F32), zs[:, :-1]], axis=1)
    t_real = bp * lp + bs * ls
    shifted = jnp.concatenate([sh_p.reshape(bp * lp, -1), sh_s.reshape(bs * ls, -1),
                               jnp.zeros((tp - t_real, zr.shape[1]), F32)], axis=0)
    zm = zr + (shifted - zr) * p['mu'].astype(F32)
    r = zm[:, :wr]
    k = zm[:, wr:2 * wr]
    v = zm[:, 2 * wr:3 * wr]
    xw = zm[:, 3 * wr:3 * wr + rk_d]
    xa = zm[:, 3 * wr + rk_d:3 * wr + rk_d + ra_d]
    xg = zm[:, 3 * wr + rk_d + ra_d:]
    w_log = -jax.nn.softplus(-(p['w0'].astype(F32) + mm(jnp.tanh(xw).astype(BF16), p['w_up']))) - 0.5
    decay = jnp.exp(-jnp.exp(w_log))
    a = jax.nn.sigmoid(p['a0'].astype(F32) + mm(xa.astype(BF16), p['a_up']))
    g = mm(jax.nn.sigmoid(xg).astype(BF16), p['g_up'])
    kk = (k * p['k_k'].astype(F32)).reshape(tp, heads, n)
    kk = kk / jnp.maximum(jnp.sqrt(jnp.sum(kk * kk, axis=-1, keepdims=True)), 1e-12)
    kk = kk.reshape(tp, wr)
    k2 = k * (1.0 + (a - 1.0) * p['k_a'].astype(F32))
    bvec = kk * a

    def run(lo, nb, nl, s0):
        hl = [_heads_on_lanes(t[lo:lo + nb * nl], nb, nl, heads, n) for t in (kk, decay, bvec, k2, r, v)]
        y, s = wkv7(*hl, s0)
        y = y[:, :, :nb * heads].reshape(nl, n, nb, heads).transpose(2, 0, 3, 1).reshape(nb * nl, wr)
        s = s[:, :, :nb * heads].reshape(n, n, nb, heads).transpose(2, 3, 0, 1)
        return y, s

    nh_p = bp * heads + (-(bp * heads)) % LANES
    y_p, s_p = run(0, bp, lp, jnp.zeros((n, n, nh_p), F32))
    s0_s = wkv0_s.astype(F32).transpose(2, 3, 0, 1).reshape(n, n, bs * heads)
    pad_s = (-(bs * heads)) % LANES
    if pad_s:
        s0_s = jnp.pad(s0_s, ((0, 0), (0, 0), (0, pad_s)))
    y_s, s_s = run(bp * lp, bs, ls, s0_s)
    y = jnp.concatenate([y_p, y_s, jnp.zeros((tp - t_real, wr), F32)], axis=0).reshape(tp, heads, n)
    mean = jnp.mean(y, axis=-1, keepdims=True)
    var = jnp.mean(jnp.square(y - mean), axis=-1, keepdims=True)
    y = ((y - mean) * lax.rsqrt(var + RWKV_LN_EPS)).reshape(tp, wr)
    y = y * p['ln_w'].astype(F32) + p['ln_b'].astype(F32)
    bonus = jnp.sum((r * k2).reshape(tp, heads, n) * p['r_k'].astype(F32), axis=-1, keepdims=True)
    bonus = bonus * v.reshape(tp, heads, n)
    y = (y + bonus.reshape(tp, wr)) * g
    return y, s_p, s_s, zp[:, -1], zs[:, -1]


def kernel(x_prompt, x_sample, state_s5_re, state_s5_im, state_rwkv_shift, state_rwkv_wkv, state_hgrn, meta_tokens, norm_mix_w, w_in, s5_lam_re, s5_lam_im, s5_log_dt, s5_b_re, s5_b_im, s5_c_re, s5_c_im, s5_d, s5_w_glu, s5_b_glu, rwkv_mu, rwkv_w0, rwkv_w_up, rwkv_a0, rwkv_a_up, rwkv_g_up, rwkv_k_k, rwkv_k_a, rwkv_r_k, rwkv_ln_w, rwkv_ln_b, hgrn_lb, hgrn_norm_w, p_s5, p_rwkv, p_hgrn, w_out, norm_ffn_w, router_g_w, router_g_b, router_e_w, router_e_b, moe_w_gate, moe_w_up, moe_w_down, final_norm_w):
    depth, D, _ = w_in.shape
    bp, seq, _ = x_prompt.shape
    bs, ls, _ = x_sample.shape
    n_meta = meta_tokens.shape[0]
    lp = seq + n_meta
    w_s5 = s5_w_glu.shape[1]
    heads_r, n_r = rwkv_r_k.shape[1], rwkv_r_k.shape[2]
    c_rwkv = rwkv_mu.shape[1]
    heads_h, dk_h = state_hgrn.shape[2], state_hgrn.shape[3]
    w_hg = heads_h * dk_h
    t_real = bp * lp + bs * ls
    tile = 1280 if t_real > 4096 else 256
    tp = -(-t_real // tile) * tile

    meta = jnp.broadcast_to(meta_tokens[None].astype(F32), (bp, n_meta, D))
    xp = jnp.concatenate([meta, x_prompt.astype(F32)], axis=1).reshape(bp * lp, D)
    h = jnp.concatenate([xp, x_sample.astype(F32).reshape(bs * ls, D), jnp.zeros((tp - t_real, D), F32)], axis=0)

    sm = jax.nn.softmax(hgrn_lb.astype(F32), axis=0)
    lower = jnp.maximum(jnp.cumsum(sm, axis=0) - sm[0], 0.0)

    outs = {k: [] for k in ('p_re', 'p_im', 'p_shift', 'p_wkv', 'p_hg', 's_re', 's_im', 's_shift', 's_wkv', 's_hg')}
    dims = dict(bp=bp, lp=lp, bs=bs, ls=ls)
    for l in range(depth):
        _, n16 = rms_norm(h, norm_mix_w[l].astype(F32))
        z = mm(n16, w_in[l])
        s5p = _s5_params(s5_lam_re[l], s5_lam_im[l], s5_log_dt[l], s5_b_re[l], s5_b_im[l],
                         s5_c_re[l], s5_c_im[l], s5_d[l])
        y_s5, pre, pim, sre, sim = _s5_branch(z[:, :w_s5], s5p, state_s5_re[l], state_s5_im[l],
                                              s5_w_glu[l], s5_b_glu[l], tp=tp, **dims)
        rp = dict(mu=rwkv_mu[l], w0=rwkv_w0[l], w_up=rwkv_w_up[l], a0=rwkv_a0[l], a_up=rwkv_a_up[l],
                  g_up=rwkv_g_up[l], k_k=rwkv_k_k[l], k_a=rwkv_k_a[l], r_k=rwkv_r_k[l],
                  ln_w=rwkv_ln_w[l], ln_b=rwkv_ln_b[l])
        y_rw, wkv_p, wkv_s, shift_p, shift_s = _rwkv_branch(
            z[:, w_s5:w_s5 + c_rwkv], state_rwkv_shift[l], state_rwkv_wkv[l], rp,
            tp=tp, heads=heads_r, n=n_r, **dims)
        o_p, o_s, hg_p, hg_s = hgrn2(z, w_s5 + c_rwkv, lower[l], hgrn_norm_w[l], state_hgrn[l],
                                     heads=heads_h, dk=dk_h, **dims)
        y_hg = jnp.concatenate([o_p, o_s, jnp.zeros((tp - t_real, w_hg), F32)], axis=0)
        g0 = w_s5 + c_rwkv + 4 * w_hg
        gates = jax.nn.sigmoid(z[:, g0:g0 + 3 * D])
        merged = (gates[:, :D] * mm(y_s5.astype(BF16), p_s5[l])
                  + gates[:, D:2 * D] * mm(y_rw.astype(BF16), p_rwkv[l])
                  + gates[:, 2 * D:] * mm(y_hg.astype(BF16), p_hgrn[l]))
        h = h + mm(merged.astype(BF16), w_out[l])
        n32, n16 = rms_norm(h, norm_ffn_w[l].astype(F32))
        h = h + moe_ffn(n32, n16, t_real, router_g_w[l], router_g_b[l], router_e_w[l], router_e_b[l],
                        moe_w_gate[l], moe_w_up[l], moe_w_down[l])
        for key, val in (('p_re', pre), ('p_im', pim), ('p_shift', shift_p), ('p_wkv', wkv_p), ('p_hg', hg_p),
                         ('s_re', sre), ('s_im', sim), ('s_shift', shift_s), ('s_wkv', wkv_s), ('s_hg', hg_s)):
            outs[key].append(val)

    y, _ = rms_norm(h, final_norm_w.astype(F32))
    yp = y[:bp * lp].reshape(bp, lp, D)[:, n_meta:]
    ys = y[bp * lp:t_real].reshape(bs, ls, D)
    st = lambda key: jnp.stack(outs[key], axis=0).astype(F32)
    return (yp, ys, st('p_re'), st('p_im'), st('p_shift'), st('p_wkv'), st('p_hg'),
            st('s_re'), st('s_im'), st('s_shift'), st('s_wkv'), st('s_hg'))
```

```python
import functools
import math

import jax
import jax.numpy as jnp
from jax import lax
from jax.experimental import pallas as pl
from jax.experimental.pallas import tpu as pltpu

F32 = jnp.float32
BF16 = jnp.bfloat16
HI = lax.Precision.HIGHEST

LANES = 128
SUBLANES = 8
VMEM_LIMIT = 56 * 1024 * 1024
RMS_EPS = 1e-6
RWKV_LN_EPS = 64e-5
TOP_K = 2
S5_GROUPS_PER_BLOCK = 8
HGRN_CHUNK = 48
HGRN_SUB = 16
HGRN_SAMPLE_SEQS = 8
MOE_ROWS = 256
MOE_TN = 256


def _cparams(n_axes):
    return pltpu.CompilerParams(dimension_semantics=("arbitrary",) * n_axes,
                                vmem_limit_bytes=VMEM_LIMIT)


def _pick(n, cands):
    for c in cands:
        if n % c == 0:
            return c
    raise ValueError(f"no tile for {n} in {cands}")


def _mm_kernel(*refs, has_res):
    if has_res:
        a_ref, w_ref, r_ref, o_ref, wc_ref = refs
    else:
        a_ref, w_ref, o_ref, wc_ref = refs
    m = pl.program_id(1)
    k = pl.program_id(2)

    @pl.when(m == 0)
    def _():
        wc_ref[k] = w_ref[0].astype(BF16)

    p = jnp.dot(a_ref[...], wc_ref[k], preferred_element_type=F32)

    @pl.when(k == 0)
    def _():
        o_ref[...] = (r_ref[...] + p) if has_res else p

    @pl.when(k > 0)
    def _():
        o_ref[...] += p


def mm(a, w, layer, res=None):
    M, K = a.shape
    N = w.shape[2]
    tm = _pick(M, (1280, 1024, 512, 256, 128))
    tn = _pick(N, (1280, 1024, 512, 256, 128))
    tk = _pick(K, (1024, 512, 256, 128, K))
    nk = K // tk
    in_specs = [pl.BlockSpec((tm, tk), lambda n, m, k: (m, k)),
                pl.BlockSpec((1, tk, tn), lambda n, m, k: (layer, jnp.where(m == 0, k, nk - 1), n))]
    args = [a, w]
    if res is not None:
        in_specs.append(pl.BlockSpec((tm, tn), lambda n, m, k: (m, n)))
        args.append(res)
    return pl.pallas_call(
        functools.partial(_mm_kernel, has_res=res is not None),
        grid=(N // tn, M // tm, nk),
        in_specs=in_specs,
        out_specs=pl.BlockSpec((tm, tn), lambda n, m, k: (m, n)),
        out_shape=jax.ShapeDtypeStruct((M, N), F32),
        scratch_shapes=[pltpu.VMEM((nk, tk, tn), BF16)],
        compiler_params=_cparams(3),
    )(*args)


def _merge_kernel(ys_ref, yr_ref, yh_ref, ps_ref, pr_ref, ph_ref, g0_ref, g1_ref, g2_ref, o_ref,
                  cs_ref, cr_ref, ch_ref):
    @pl.when(pl.program_id(1) == 0)
    def _():
        cs_ref[...] = ps_ref[0].astype(BF16)
        cr_ref[...] = pr_ref[0].astype(BF16)
        ch_ref[...] = ph_ref[0].astype(BF16)

    acc = (jax.nn.sigmoid(g0_ref[...]) * jnp.dot(ys_ref[...], cs_ref[...], preferred_element_type=F32)
           + jax.nn.sigmoid(g1_ref[...]) * jnp.dot(yr_ref[...], cr_ref[...], preferred_element_type=F32)
           + jax.nn.sigmoid(g2_ref[...]) * jnp.dot(yh_ref[...], ch_ref[...], preferred_element_type=F32))
    o_ref[...] = acc.astype(BF16)


def merge_branches(z, gate_col0, ys, yr, yh, p_s5, p_rwkv, p_hgrn, layer):
    M = z.shape[0]
    D = p_s5.shape[2]
    tm = _pick(M, (1280, 1024, 512, 256, 128))
    tn = _pick(math.gcd(D, gate_col0), (256, 128))
    gb = gate_col0 // tn
    nd = D // tn

    def aspec(y):
        return pl.BlockSpec((tm, y.shape[1]), lambda n, m: (m, 0))

    def wspec(p):
        return pl.BlockSpec((1, p.shape[1], tn), lambda n, m: (layer, 0, n))

    def gspec(i):
        return pl.BlockSpec((tm, tn), lambda n, m: (m, gb + i * nd + n))

    return pl.pallas_call(
        _merge_kernel,
        grid=(nd, M // tm),
        in_specs=[aspec(ys), aspec(yr), aspec(yh), wspec(p_s5), wspec(p_rwkv), wspec(p_hgrn),
                  gspec(0), gspec(1), gspec(2)],
        out_specs=pl.BlockSpec((tm, tn), lambda n, m: (m, n)),
        out_shape=jax.ShapeDtypeStruct((M, D), BF16),
        scratch_shapes=[pltpu.VMEM((p.shape[1], tn), BF16) for p in (p_s5, p_rwkv, p_hgrn)],
        compiler_params=_cparams(2),
    )(ys, yr, yh, p_s5, p_rwkv, p_hgrn, z, z, z)


def _router_kernel(n_ref, w_ref, b_ref, o_ref, *, n_grp, n_exp):
    e_per = n_exp // n_grp
    logits = jnp.dot(n_ref[...], w_ref[...], precision=HI, preferred_element_type=F32) + b_ref[...]
    lane = lax.broadcasted_iota(jnp.int32, logits.shape, 1)
    lane_f = lane.astype(F32)
    big = float(4 * LANES)
    neg = -jnp.inf
    is_g = lane < n_grp
    lg = jnp.where(is_g, logits, neg)
    eg = jnp.exp(lg - jnp.max(lg, axis=-1, keepdims=True))
    pg = eg / jnp.sum(eg, axis=-1, keepdims=True)
    p_g = jnp.max(pg, axis=-1, keepdims=True)
    g_idx = jnp.min(jnp.where(is_g & (pg == p_g), lane_f, big), axis=-1, keepdims=True)
    e_f = lane_f - float(n_grp)
    in_sel = (e_f >= g_idx * e_per) & (e_f < (g_idx + 1.0) * e_per)
    le = jnp.where(in_sel, logits, neg)
    ee = jnp.exp(le - jnp.max(le, axis=-1, keepdims=True))
    pe = ee / jnp.sum(ee, axis=-1, keepdims=True)
    p1 = jnp.max(pe, axis=-1, keepdims=True)
    i1 = jnp.min(jnp.where(in_sel & (pe == p1), e_f, big), axis=-1, keepdims=True)
    rest = in_sel & (e_f != i1)
    p2 = jnp.max(jnp.where(rest, pe, -1.0), axis=-1, keepdims=True)
    i2 = jnp.min(jnp.where(rest & (pe == p2), e_f, big), axis=-1, keepdims=True)
    den = p1 + p2
    out = jnp.where(lane == 0, p_g * p1 / den,
                    jnp.where(lane == 1, p_g * p2 / den,
                              jnp.where(lane == 2, i1, jnp.where(lane == 3, i2, 0.0))))
    o_ref[...] = out


def router(n32, w, b, *, n_grp, n_exp):
    M, K = n32.shape
    tm = _pick(M, (256, 128))
    return pl.pallas_call(
        functools.partial(_router_kernel, n_grp=n_grp, n_exp=n_exp),
        grid=(M // tm,),
        in_specs=[pl.BlockSpec((tm, K), lambda m: (m, 0)),
                  pl.BlockSpec((K, LANES), lambda m: (0, 0)),
                  pl.BlockSpec((1, LANES), lambda m: (0, 0))],
        out_specs=pl.BlockSpec((tm, LANES), lambda m: (m, 0)),
        out_shape=jax.ShapeDtypeStruct((M, LANES), F32),
        compiler_params=_cparams(1),
    )(n32, w, b)


def _rms_kernel(x_ref, w_ref, *o_refs, want32, want16):
    x = x_ref[...]
    y = x * lax.rsqrt(jnp.mean(x * x, axis=-1, keepdims=True) + RMS_EPS) * w_ref[...]
    if want32:
        o_refs[0][...] = y
    if want16:
        o_refs[-1][...] = y.astype(BF16)


def rms_norm(x, w, *, want32, want16):
    M, D = x.shape
    tm = _pick(M, (256, 128))
    dts = ([F32] if want32 else []) + ([BF16] if want16 else [])
    return pl.pallas_call(
        functools.partial(_rms_kernel, want32=want32, want16=want16),
        grid=(M // tm,),
        in_specs=[pl.BlockSpec((tm, D), lambda m: (m, 0)),
                  pl.BlockSpec((1, D), lambda m: (0, 0))],
        out_specs=[pl.BlockSpec((tm, D), lambda m: (m, 0)) for _ in dts],
        out_shape=[jax.ShapeDtypeStruct((M, D), dt) for dt in dts],
        compiler_params=_cparams(1),
    )(x, w.reshape(1, D))


def _gelu_tanh(y):
    return 0.5 * y * (1.0 + jnp.tanh(math.sqrt(2.0 / math.pi) * (y + 0.044715 * (y * y * y))))


def _s5_kernel(u_ref, bm_ref, cm_ref, are_ref, aim_ref, d_ref, x0re_ref, x0im_ref,
               y_ref, xlre_ref, xlim_ref, xs_ref, sre_ref, sim_ref, *, steps, rows, half):
    tc = pl.program_id(2)

    @pl.when(tc == 0)
    def _():
        sre_ref[...] = x0re_ref[...]
        sim_ref[...] = x0im_ref[...]

    u = u_ref[...]
    bm = bm_ref[0]
    u_hi = u.astype(BF16)
    u_lo = (u - u_hi.astype(F32)).astype(BF16)
    b_hi = bm.astype(BF16)
    b_lo = (bm - b_hi.astype(F32)).astype(BF16)
    xs_ref[...] = (jnp.dot(u_hi, b_hi, preferred_element_type=F32)
                   + jnp.dot(u_hi, b_lo, preferred_element_type=F32)
                   + jnp.dot(u_lo, b_hi, preferred_element_type=F32))
    a_re = are_ref[0]
    a_im = aim_ref[0]

    def step(t, carry):
        x_re, x_im = carry
        r0 = pl.multiple_of(t * rows, rows)
        b_re = xs_ref[pl.ds(r0, rows), 0:half]
        b_im = xs_ref[pl.ds(r0, rows), half:2 * half]
        n_re = a_re * x_re - a_im * x_im + b_re
        n_im = a_re * x_im + a_im * x_re + b_im
        xs_ref[pl.ds(r0, rows), 0:half] = n_re
        xs_ref[pl.ds(r0, rows), half:2 * half] = n_im
        return n_re, n_im

    x_re, x_im = lax.fori_loop(0, steps, step, (sre_ref[...], sim_ref[...]), unroll=min(8, steps))
    sre_ref[...] = x_re
    sim_ref[...] = x_im
    xlre_ref[...] = x_re
    xlim_ref[...] = x_im
    y = jnp.dot(xs_ref[...].astype(BF16), cm_ref[0].astype(BF16), preferred_element_type=F32) + d_ref[0] * u
    y_ref[...] = _gelu_tanh(y)


def s5_scan(u2, bm, cm, a_re, a_im, dvec, x0_re, x0_im, *, groups, ntc, steps, rows):
    n_rows, W = u2.shape
    gbn = W // LANES
    half = bm.shape[2] // 2
    blk = steps * rows
    kern = functools.partial(_s5_kernel, steps=steps, rows=rows, half=half)
    return pl.pallas_call(
        kern,
        grid=(gbn, groups, ntc),
        in_specs=[pl.BlockSpec((blk, LANES), lambda g, s, t: (s * ntc + t, g)),
                  pl.BlockSpec((1, LANES, 2 * half), lambda g, s, t: (g, 0, 0)),
                  pl.BlockSpec((1, 2 * half, LANES), lambda g, s, t: (g, 0, 0)),
                  pl.BlockSpec((1, 1, half), lambda g, s, t: (g, 0, 0)),
                  pl.BlockSpec((1, 1, half), lambda g, s, t: (g, 0, 0)),
                  pl.BlockSpec((1, 1, LANES), lambda g, s, t: (g, 0, 0)),
                  pl.BlockSpec((rows, half), lambda g, s, t: (s, g)),
                  pl.BlockSpec((rows, half), lambda g, s, t: (s, g))],
        out_specs=[pl.BlockSpec((blk, LANES), lambda g, s, t: (s * ntc + t, g)),
                   pl.BlockSpec((rows, half), lambda g, s, t: (s, g)),
                   pl.BlockSpec((rows, half), lambda g, s, t: (s, g))],
        out_shape=[jax.ShapeDtypeStruct((n_rows, W), F32),
                   jax.ShapeDtypeStruct(x0_re.shape, F32),
                   jax.ShapeDtypeStruct(x0_im.shape, F32)],
        scratch_shapes=[pltpu.VMEM((blk, 2 * half), F32),
                        pltpu.VMEM((rows, half), F32),
                        pltpu.VMEM((rows, half), F32)],
        compiler_params=_cparams(3),
    )(u2, bm, cm, a_re, a_im, dvec, x0_re, x0_im)


def _s5_params(lam_re, lam_im, log_dt, b_re, b_im, c_re, c_im, d_skip):
    G, P, H = b_re.shape
    gpb = S5_GROUPS_PER_BLOCK
    gbn = G // gpb
    lam = lax.complex(lam_re.astype(F32), lam_im.astype(F32))
    dt = jnp.exp(log_dt.astype(F32))[:, None]
    a_bar = jnp.exp(lam * dt)
    b_bar = ((a_bar - 1.0) / lam)[..., None] * lax.complex(b_re.astype(F32), b_im.astype(F32))
    eye = jnp.eye(gpb, dtype=F32)

    def bdiag_in(m):
        m = m.reshape(gbn, gpb, P, H).transpose(0, 1, 3, 2)
        return jnp.einsum('aghp,gk->aghkp', m, eye).reshape(gbn, gpb * H, gpb * P)

    def bdiag_out(m):
        m = m.reshape(gbn, gpb, H, P)
        return jnp.einsum('aghp,gk->agpkh', m, eye).reshape(gbn, gpb * P, gpb * H)

    bm = jnp.concatenate([bdiag_in(jnp.real(b_bar)), bdiag_in(jnp.imag(b_bar))], axis=2)
    cm = jnp.concatenate([bdiag_out(c_re.astype(F32)), bdiag_out(-c_im.astype(F32))], axis=1)
    a_re = jnp.real(a_bar).reshape(gbn, 1, gpb * P)
    a_im = jnp.imag(a_bar).reshape(gbn, 1, gpb * P)
    dvec = d_skip.astype(F32).reshape(gbn, 1, gpb * H)
    return bm, cm, a_re, a_im, dvec


def _wkv_kernel(kk_ref, w_ref, b_ref, k_ref, r_ref, v_ref, s0_ref, y_ref, so_ref, s_ref, sa_ref, *, steps, n):
    tc = pl.program_id(1)

    @pl.when(tc == 0)
    def _():
        s_ref[...] = s0_ref[...]

    def tstep(t, _):
        kk = kk_ref[t]

        def sa_pass(v, _):
            sa_ref[pl.ds(v, 1), :] = -jnp.sum(s_ref[v] * kk, axis=0, keepdims=True)
            return 0

        lax.fori_loop(0, n, sa_pass, 0, unroll=8)
        w = w_ref[t]
        b = b_ref[t]
        kx = k_ref[t]
        r = r_ref[t]

        def update_pass(v, _):
            sn = s_ref[v] * w + sa_ref[pl.ds(v, 1), :] * b + v_ref[t, pl.ds(v, 1), :] * kx
            s_ref[v] = sn
            y_ref[t, pl.ds(v, 1), :] = jnp.sum(sn * r, axis=0, keepdims=True)
            return 0

        lax.fori_loop(0, n, update_pass, 0, unroll=8)
        return 0

    lax.fori_loop(0, steps, tstep, 0)
    so_ref[...] = s_ref[...]


def wkv7(kk, w, b, k, r, v, s0):
    NT, N, NH = kk.shape
    steps = _pick(NT, (48, 16, 8, 4, 2, 1))
    seq_spec = pl.BlockSpec((steps, N, LANES), lambda h, t: (t, 0, h))
    st_spec = pl.BlockSpec((N, N, LANES), lambda h, t: (0, 0, h))
    kern = functools.partial(_wkv_kernel, steps=steps, n=N)
    return pl.pallas_call(
        kern,
        grid=(NH // LANES, NT // steps),
        in_specs=[seq_spec] * 6 + [st_spec],
        out_specs=[seq_spec, st_spec],
        out_shape=[jax.ShapeDtypeStruct((NT, N, NH), F32), jax.ShapeDtypeStruct((N, N, NH), F32)],
        scratch_shapes=[pltpu.VMEM((N, N, LANES), F32), pltpu.VMEM((N, LANES), F32)],
        compiler_params=_cparams(2),
    )(kk, w, b, k, r, v, s0)


def _head_sum(x, e_ref):
    hi = x.astype(BF16)
    lo = (x - hi.astype(F32)).astype(BF16)
    e = e_ref[...]
    return jnp.dot(hi, e, preferred_element_type=F32) + jnp.dot(lo, e, preferred_element_type=F32)


def _rwkv_prep_kernel(k_ref, xw_ref, xa_ref, xg0_ref, xg1_ref, wup_ref, aup_ref, gup_ref,
                      w0_ref, a0_ref, kk_w_ref, ka_ref, e_ref,
                      kk_o, w_o, b_o, k_o, g_o):
    k = k_ref[...]
    lw = w0_ref[...] + jnp.dot(jnp.tanh(xw_ref[...]).astype(BF16), wup_ref[0].astype(BF16),
                               preferred_element_type=F32)
    w_log = -(jnp.maximum(-lw, 0.0) + jnp.log1p(jnp.exp(-jnp.abs(lw)))) - 0.5
    w_o[...] = jnp.exp(-jnp.exp(w_log))
    a = jax.nn.sigmoid(a0_ref[...] + jnp.dot(xa_ref[...].astype(BF16), aup_ref[0].astype(BF16),
                                             preferred_element_type=F32))
    xg = jnp.concatenate([xg0_ref[...], xg1_ref[...]], axis=1)
    g_o[...] = jnp.dot(jax.nn.sigmoid(xg).astype(BF16), gup_ref[0].astype(BF16), preferred_element_type=F32)
    kx = k * kk_w_ref[...]
    kk = kx / jnp.maximum(jnp.sqrt(_head_sum(kx * kx, e_ref)), 1e-12)
    kk_o[...] = kk
    b_o[...] = kk * a
    k_o[...] = k * (1.0 + (a - 1.0) * ka_ref[...])


def rwkv_prep(zm, p, layer, e_mat, *, wr):
    Tp = zm.shape[0]
    half = e_mat.shape[0]
    nh = wr // half
    rk_d = p['w_up'].shape[1]
    ra_d = p['a_up'].shape[1]
    rg_d = p['g_up'].shape[1]
    assert rk_d == LANES and ra_d == LANES and rg_d % (2 * LANES) == 0 and wr % half == 0
    tm = _pick(Tp, (256, 128))
    xw_blk = 3 * wr // LANES
    xg_blk = (3 * wr + rk_d + ra_d) // (rg_d // 2)
    assert (3 * wr + rk_d + ra_d) % (rg_d // 2) == 0

    def zspec(sec):
        return pl.BlockSpec((tm, half), lambda i, hf: (i, sec * nh + hf))

    def vec(x):
        return x.astype(F32).reshape(1, wr)

    vspec = pl.BlockSpec((1, half), lambda i, hf: (0, hf))
    out_spec = pl.BlockSpec((tm, half), lambda i, hf: (i, hf))
    outs = pl.pallas_call(
        _rwkv_prep_kernel,
        grid=(Tp // tm, nh),
        in_specs=[zspec(1),
                  pl.BlockSpec((tm, LANES), lambda i, hf: (i, xw_blk)),
                  pl.BlockSpec((tm, LANES), lambda i, hf: (i, xw_blk + 1)),
                  pl.BlockSpec((tm, rg_d // 2), lambda i, hf: (i, xg_blk)),
                  pl.BlockSpec((tm, rg_d // 2), lambda i, hf: (i, xg_blk + 1)),
                  pl.BlockSpec((1, rk_d, half), lambda i, hf: (layer, 0, hf)),
                  pl.BlockSpec((1, ra_d, half), lambda i, hf: (layer, 0, hf)),
                  pl.BlockSpec((1, rg_d, half), lambda i, hf: (layer, 0, hf)),
                  vspec, vspec, vspec, vspec,
                  pl.BlockSpec((half, half), lambda i, hf: (0, 0))],
        out_specs=[out_spec] * 5,
        out_shape=[jax.ShapeDtypeStruct((Tp, wr), F32)] * 5,
        compiler_params=_cparams(2),
    )(zm, zm, zm, zm, zm, p['w_up'], p['a_up'], p['g_up'],
      vec(p['w0']), vec(p['a0']), vec(p['k_k']), vec(p['k_a']), e_mat)
    return outs


def _rwkv_post_kernel(y_ref, r_ref, k_ref, v_ref, g_ref, lnw_ref, lnb_ref, rk_ref, e_ref, o_ref, *, n):
    y = y_ref[...]
    d = y - _head_sum(y, e_ref) * (1.0 / n)
    var = _head_sum(d * d, e_ref) * (1.0 / n)
    yn = d * lax.rsqrt(var + RWKV_LN_EPS) * lnw_ref[...] + lnb_ref[...]
    bonus = _head_sum(r_ref[...] * k_ref[...] * rk_ref[...], e_ref) * v_ref[...]
    o_ref[...] = ((yn + bonus) * g_ref[...]).astype(BF16)


def rwkv_post(y, zm, k2, g, p, e_mat, *, wr, n):
    Tp = y.shape[0]
    half = e_mat.shape[0]
    nh = wr // half
    tm = _pick(Tp, (256, 128))
    blk = pl.BlockSpec((tm, half), lambda i, hf: (i, hf))
    vspec = pl.BlockSpec((1, half), lambda i, hf: (0, hf))

    def vec(x):
        return x.astype(F32).reshape(1, wr)

    return pl.pallas_call(
        functools.partial(_rwkv_post_kernel, n=n),
        grid=(Tp // tm, nh),
        in_specs=[blk,
                  pl.BlockSpec((tm, half), lambda i, hf: (i, hf)),
                  blk,
                  pl.BlockSpec((tm, half), lambda i, hf: (i, 2 * nh + hf)),
                  blk, vspec, vspec, vspec,
                  pl.BlockSpec((half, half), lambda i, hf: (0, 0))],
        out_specs=blk,
        out_shape=jax.ShapeDtypeStruct((Tp, wr), BF16),
        compiler_params=_cparams(2),
    )(y, zm, k2, zm, g, vec(p['ln_w']), vec(p['ln_b']), vec(p['r_k']), e_mat)


def _hgrn_gates(zq, zf, loglb, log1mlb, onemlb):
    q = zq * jax.nn.sigmoid(zq)
    log_sig = jnp.minimum(zf, 0.0) - jnp.log1p(jnp.exp(-jnp.abs(zf)))
    t = log1mlb + log_sig
    hi = jnp.maximum(loglb, t)
    log_f = hi + jnp.log1p(jnp.exp(-jnp.abs(loglb - t)))
    k = onemlb * jax.nn.sigmoid(-zf)
    return q, log_f, k


def _hgrn_out(o, zg, nw):
    o = o * lax.rsqrt(jnp.mean(o * o, axis=-1, keepdims=True) + RMS_EPS)
    return o * nw * (zg * jax.nn.sigmoid(zg))


def _col_bcast(row, n_lanes):
    e = jnp.concatenate([row, jnp.zeros((SUBLANES - 1, row.shape[1]), F32)], axis=0)
    ones = jnp.concatenate([jnp.ones((1, n_lanes), F32), jnp.zeros((SUBLANES - 1, n_lanes), F32)], axis=0)
    return lax.dot_general(e, ones, (((0,), (0,)), ((), ())), precision=HI, preferred_element_type=F32)


def _hgrn_prompt_kernel(*refs, rows, sub, heads):
    z_refs = refs[:4 * heads]
    loglb_ref, log1mlb_ref, onemlb_ref, nw_ref, o_ref, so_ref, s_ref = refs[4 * heads:]

    @pl.when(pl.program_id(1) == 0)
    def _():
        s_ref[...] = jnp.zeros_like(s_ref)

    dk = s_ref.shape[1]
    for h in range(heads):
        zq_ref, zf_ref, zi_ref, zg_ref = (z_refs[sec * heads + h] for sec in range(4))
        o, s_new = _hgrn_chunk(zq_ref[...], zf_ref[...], zi_ref[...], loglb_ref[h], log1mlb_ref[h],
                               onemlb_ref[h], s_ref[h], rows=rows, sub=sub)
        s_ref[h] = s_new
        so_ref[0, h] = s_new
        o_ref[:, h * dk:(h + 1) * dk] = _hgrn_out(o, zg_ref[...], nw_ref[h])


def _hgrn_chunk(zq, zf, v, loglb, log1mlb, onemlb, s_old, *, rows, sub):
    q, log_f, k = _hgrn_gates(zq, zf, loglb, log1mlb, onemlb)
    dv = v.shape[1]
    ri = lax.broadcasted_iota(jnp.int32, (rows, rows), 0)
    ci = lax.broadcasted_iota(jnp.int32, (rows, rows), 1)
    tri = (ci <= ri).astype(F32)
    bcum = jnp.dot(tri, log_f, precision=HI, preferred_element_type=F32)
    o_parts = []
    for i in range(rows // sub):
        lo, hi = i * sub, (i + 1) * sub
        b_i = bcum[lo:hi]
        q_i = q[lo:hi]
        o_i = jnp.dot(q_i * jnp.exp(b_i), s_old, precision=HI, preferred_element_type=F32)
        if i > 0:
            b_ref_row = bcum[lo - 1:lo]
            qe = q_i * jnp.exp(b_i - b_ref_row)
            ke = k[0:lo] * jnp.exp(b_ref_row - bcum[0:lo])
            att = lax.dot_general(qe, ke, (((1,), (1,)), ((), ())), precision=HI, preferred_element_type=F32)
            o_i = o_i + jnp.dot(att, v[0:lo], precision=HI, preferred_element_type=F32)
        t_idx = lax.broadcasted_iota(jnp.int32, (sub, 1), 0)
        for s in range(sub):
            e_s = jnp.exp(jnp.minimum(b_i - b_i[s:s + 1], 0.0))
            col = jnp.sum(q_i * k[lo + s:lo + s + 1] * e_s, axis=-1, keepdims=True)
            col = jnp.where(t_idx >= s, col, 0.0)
            o_i = o_i + col * v[lo + s:lo + s + 1]
        o_parts.append(o_i)
    o = jnp.concatenate(o_parts, axis=0)
    b_end = bcum[rows - 1:rows]
    ke_end = k * jnp.exp(b_end - bcum)
    s_new = (_col_bcast(jnp.exp(b_end), dv) * s_old
             + lax.dot_general(ke_end, v, (((0,), (0,)), ((), ())), precision=HI, preferred_element_type=F32))
    return o, s_new


def _hgrn_sample_kernel(zq_ref, zf_ref, zi_ref, zg_ref, loglb_ref, log1mlb_ref, onemlb_ref, nw_ref, s0_ref,
                        o_ref, so_ref, *, nseq, slen):
    rows = nseq * slen
    q, log_f, k = _hgrn_gates(zq_ref[...], zf_ref[...], loglb_ref[0], log1mlb_ref[0], onemlb_ref[0])
    v = zi_ref[...]
    dv = v.shape[1]
    ri = lax.broadcasted_iota(jnp.int32, (rows, rows), 0)
    ci = lax.broadcasted_iota(jnp.int32, (rows, rows), 1)
    same = (ri // slen) == (ci // slen)
    tri = ((ci <= ri) & same).astype(F32)
    bcum = jnp.dot(tri, log_f, precision=HI, preferred_element_type=F32)
    t_idx = lax.broadcasted_iota(jnp.int32, (rows, 1), 0)
    o = jnp.zeros((rows, dv), F32)
    for s in range(rows):
        e_s = jnp.exp(jnp.minimum(bcum - bcum[s:s + 1], 0.0))
        col = jnp.sum(q * k[s:s + 1] * e_s, axis=-1, keepdims=True)
        col = jnp.where((t_idx >= s) & (t_idx < (s // slen + 1) * slen), col, 0.0)
        o = o + col * v[s:s + 1]
    qe = q * jnp.exp(bcum)
    for j in range(nseq):
        in_seq = (t_idx >= j * slen) & (t_idx < (j + 1) * slen)
        s_old = s0_ref[j, 0]
        o = o + jnp.dot(jnp.where(in_seq, qe, 0.0), s_old, precision=HI, preferred_element_type=F32)
        b_end = bcum[(j + 1) * slen - 1:(j + 1) * slen]
        ke = jnp.where(in_seq, k * jnp.exp(b_end - bcum), 0.0)
        so_ref[j, 0] = (_col_bcast(jnp.exp(b_end), dv) * s_old
                        + lax.dot_general(ke, v, (((0,), (0,)), ((), ())), precision=HI,
                                          preferred_element_type=F32))
    o_ref[...] = _hgrn_out(o, zg_ref[...], nw_ref[0])


def hgrn2(z, col0, lb, norm_w, s0_sample, *, bp, lp, bs, ls, heads, dk):
    assert dk == LANES and col0 % LANES == 0
    cb = col0 // LANES
    loglb = jnp.log(lb).reshape(heads, 1, dk)
    log1mlb = jnp.log1p(-lb).reshape(heads, 1, dk)
    onemlb = (1.0 - lb).reshape(heads, 1, dk)
    nw = norm_w.astype(F32).reshape(heads, 1, dk)
    W = heads * dk
    par_full = pl.BlockSpec((heads, 1, dk), lambda b, c: (0, 0, 0))

    rc = HGRN_CHUNK
    assert lp % rc == 0
    nch = lp // rc

    def zspec(col):
        return pl.BlockSpec((rc, dk), lambda b, c: (b * nch + c, cb + col))

    o_p, s_p = pl.pallas_call(
        functools.partial(_hgrn_prompt_kernel, rows=rc, sub=HGRN_SUB, heads=heads),
        grid=(bp, nch),
        in_specs=[zspec(col) for col in range(4 * heads)] + [par_full] * 4,
        out_specs=[pl.BlockSpec((rc, W), lambda b, c: (b * nch + c, 0)),
                   pl.BlockSpec((1, heads, dk, dk), lambda b, c: (b, 0, 0, 0))],
        out_shape=[jax.ShapeDtypeStruct((bp * lp, W), F32),
                   jax.ShapeDtypeStruct((bp, heads, dk, dk), F32)],
        scratch_shapes=[pltpu.VMEM((heads, dk, dk), F32)],
        compiler_params=_cparams(2),
    )(*([z] * (4 * heads)), loglb, log1mlb, onemlb, nw)

    nseq = HGRN_SAMPLE_SEQS
    rs = nseq * ls
    assert bs % nseq == 0 and rs % SUBLANES == 0 and (bp * lp) % rs == 0
    r0 = bp * lp // rs
    par_spec2 = pl.BlockSpec((1, 1, dk), lambda p, h: (h, 0, 0))

    def zspec_s(sec):
        return pl.BlockSpec((rs, dk), lambda p, h: (r0 + p, cb + sec * heads + h))

    st_spec = pl.BlockSpec((nseq, 1, dk, dk), lambda p, h: (p, h, 0, 0))
    o_s, s_s = pl.pallas_call(
        functools.partial(_hgrn_sample_kernel, nseq=nseq, slen=ls),
        grid=(bs // nseq, heads),
        in_specs=[zspec_s(0), zspec_s(1), zspec_s(2), zspec_s(3), par_spec2, par_spec2, par_spec2, par_spec2,
                  st_spec],
        out_specs=[pl.BlockSpec((rs, dk), lambda p, h: (p, h)), st_spec],
        out_shape=[jax.ShapeDtypeStruct((bs * ls, W), F32),
                   jax.ShapeDtypeStruct((bs, heads, dk, dk), F32)],
        compiler_params=_cparams(2),
    )(z, z, z, z, loglb, log1mlb, onemlb, nw, s0_sample.astype(F32))
    return o_p, o_s, s_p, s_s


def _moe_up_kernel(be_ref, nb_ref, x_ref, wg_ref, wu_ref, h_ref, wgc_ref, wuc_ref):
    i = pl.program_id(1)
    valid = i < nb_ref[0]
    prev = be_ref[jnp.maximum(i - 1, 0)]
    fresh = valid & ((i == 0) | (be_ref[i] != prev))

    @pl.when(fresh)
    def _():
        wgc_ref[...] = wg_ref[0, 0].astype(BF16)
        wuc_ref[...] = wu_ref[0, 0].astype(BF16)

    @pl.when(valid)
    def _():
        x = x_ref[...]
        g = jnp.dot(x, wgc_ref[...], preferred_element_type=F32)
        u = jnp.dot(x, wuc_ref[...], preferred_element_type=F32)
        h_ref[...] = (g * jax.nn.sigmoid(g) * u).astype(BF16)

    @pl.when(jnp.logical_not(valid))
    def _():
        h_ref[...] = jnp.zeros_like(h_ref)


def _moe_down_kernel(be_ref, nb_ref, h_ref, wd_ref, y_ref, wdc_ref):
    i = pl.program_id(1)
    valid = i < nb_ref[0]
    prev = be_ref[jnp.maximum(i - 1, 0)]
    fresh = valid & ((i == 0) | (be_ref[i] != prev))

    @pl.when(fresh)
    def _():
        wdc_ref[...] = wd_ref[0, 0].astype(BF16)

    @pl.when(valid)
    def _():
        y_ref[...] = jnp.dot(h_ref[...], wdc_ref[...], preferred_element_type=F32)

    @pl.when(jnp.logical_not(valid))
    def _():
        y_ref[...] = jnp.zeros_like(y_ref)


def moe_experts(xs, blk_expert, n_used, w_gate, w_up, w_down, layer):
    R, D = xs.shape
    DE = w_gate.shape[3]
    nblk = R // MOE_ROWS
    tn = _pick(DE, (MOE_TN, 128))
    hid = pl.pallas_call(
        _moe_up_kernel,
        grid_spec=pltpu.PrefetchScalarGridSpec(
            num_scalar_prefetch=2,
            grid=(DE // tn, nblk),
            in_specs=[pl.BlockSpec((MOE_ROWS, D), lambda j, i, be, nb: (jnp.minimum(i, nb[0] - 1), 0)),
                      pl.BlockSpec((1, 1, D, tn), lambda j, i, be, nb: (layer, be[i], 0, j)),
                      pl.BlockSpec((1, 1, D, tn), lambda j, i, be, nb: (layer, be[i], 0, j))],
            out_specs=pl.BlockSpec((MOE_ROWS, tn), lambda j, i, be, nb: (i, j)),
            scratch_shapes=[pltpu.VMEM((D, tn), BF16), pltpu.VMEM((D, tn), BF16)]),
        out_shape=jax.ShapeDtypeStruct((R, DE), BF16),
        compiler_params=_cparams(2),
    )(blk_expert, n_used, xs, w_gate, w_up)
    tn2 = _pick(D, (1024, 512, 256, 128))
    return pl.pallas_call(
        _moe_down_kernel,
        grid_spec=pltpu.PrefetchScalarGridSpec(
            num_scalar_prefetch=2,
            grid=(D // tn2, nblk),
            in_specs=[pl.BlockSpec((MOE_ROWS, DE), lambda j, i, be, nb: (jnp.minimum(i, nb[0] - 1), 0)),
                      pl.BlockSpec((1, 1, DE, tn2), lambda j, i, be, nb: (layer, be[i], 0, j))],
            out_specs=pl.BlockSpec((MOE_ROWS, tn2), lambda j, i, be, nb: (i, j)),
            scratch_shapes=[pltpu.VMEM((DE, tn2), BF16)]),
        out_shape=jax.ShapeDtypeStruct((R, D), F32),
        compiler_params=_cparams(2),
    )(blk_expert, n_used, hid, w_down)


def moe_ffn(n32, n16, t_real, rg_w, rg_b, re_w, re_b, w_gate, w_up, w_down, layer):
    Tp, D = n32.shape
    n_grp = rg_w.shape[1]
    n_exp = re_w.shape[1]
    wr = jnp.concatenate([rg_w, re_w], axis=1).astype(F32)
    wr = jnp.pad(wr, ((0, 0), (0, LANES - wr.shape[1])))
    br = jnp.concatenate([rg_b, re_b]).astype(F32)
    br = jnp.pad(br, (0, LANES - br.shape[0])).reshape(1, LANES)
    routed = router(n32, wr, br, n_grp=n_grp, n_exp=n_exp)[:t_real]
    gate = routed[:, 0:TOP_K]
    expert = routed[:, TOP_K:2 * TOP_K].astype(jnp.int32)

    tk = t_real * TOP_K
    flat_e = expert.reshape(tk).astype(jnp.int32)
    order = jnp.argsort(flat_e)
    e_sorted = flat_e[order]
    counts = jnp.bincount(flat_e, length=n_exp)
    padded = (counts + MOE_ROWS - 1) // MOE_ROWS * MOE_ROWS
    pad_end = jnp.cumsum(padded)
    pad_start = pad_end - padded
    start = jnp.cumsum(counts) - counts
    dest = (pad_start[e_sorted] + jnp.arange(tk, dtype=jnp.int32) - start[e_sorted]).astype(jnp.int32)
    n_blocks = -(-tk // MOE_ROWS) + n_exp
    R = n_blocks * MOE_ROWS
    row_tok = jnp.zeros((R,), jnp.int32).at[dest].set((order // TOP_K).astype(jnp.int32))
    n_used = (pad_end[-1] // MOE_ROWS).astype(jnp.int32)
    blk_start = jnp.arange(n_blocks, dtype=jnp.int32) * MOE_ROWS
    blk_start = jnp.minimum(blk_start, (n_used - 1) * MOE_ROWS)
    blk_expert = jnp.minimum(jnp.searchsorted(pad_end, blk_start, side='right'), n_exp - 1).astype(jnp.int32)
    xs = n16[row_tok]
    yb = moe_experts(xs, blk_expert, n_used.reshape(1), w_gate, w_up, w_down, layer)
    slot = jnp.zeros((tk,), jnp.int32).at[order].set(dest).reshape(t_real, TOP_K)
    y = yb[slot[:, 0]] * gate[:, 0:1] + yb[slot[:, 1]] * gate[:, 1:2]
    return jnp.pad(y, ((0, Tp - t_real), (0, 0)))


def _split_groups(x, bp, lp, bs, ls):
    n = x.shape[-1]
    xp = x[:bp * lp].reshape(bp, lp, n)
    xs = x[bp * lp:bp * lp + bs * ls].reshape(bs, ls, n)
    return xp, xs


def _s5_branch(u, prm, x0re_s, x0im_s, w_glu, b_glu, layer, *, bp, lp, bs, ls, tp):
    W = u.shape[1]
    bm, cm, a_re, a_im, dvec = prm
    n_state = x0re_s.shape[1] * x0re_s.shape[2]
    up, us = _split_groups(u, bp, lp, bs, ls)
    rows_p = SUBLANES
    steps_p = _pick(lp, (344, 48, 16, 8))
    up_tm = jnp.pad(up, ((0, rows_p - bp), (0, 0), (0, 0))).transpose(1, 0, 2).reshape(lp * rows_p, W)
    zero = jnp.zeros((rows_p, n_state), F32)
    yp, pre, pim = s5_scan(up_tm, bm, cm, a_re, a_im, dvec, zero, zero,
                           groups=1, ntc=lp // steps_p, steps=steps_p, rows=rows_p)
    yp = yp.reshape(lp, rows_p, W).transpose(1, 0, 2)[:bp].reshape(bp * lp, W)
    rows_s = _pick(bs, (32, 16, 8))
    ng = bs // rows_s
    us_tm = us.reshape(ng, rows_s, ls, W).transpose(0, 2, 1, 3).reshape(bs * ls, W)
    ys, sre, sim = s5_scan(us_tm, bm, cm, a_re, a_im, dvec,
                           x0re_s.reshape(bs, n_state).astype(F32), x0im_s.reshape(bs, n_state).astype(F32),
                           groups=ng, ntc=1, steps=ls, rows=rows_s)
    ys = ys.reshape(ng, ls, rows_s, W).transpose(0, 2, 1, 3).reshape(bs * ls, W)
    y = jnp.concatenate([yp, ys, jnp.zeros((tp - bp * lp - bs * ls, W), F32)], axis=0)
    gl = mm(y.astype(BF16), w_glu, layer)
    y = y * jax.nn.sigmoid(gl + b_glu.astype(F32))
    gshape = x0re_s.shape[1:]
    return (y, pre[:bp].reshape((bp,) + gshape), pim[:bp].reshape((bp,) + gshape),
            sre.reshape((bs,) + gshape), sim.reshape((bs,) + gshape))


def _heads_on_lanes(x, nb, nl, heads, n):
    x = x.reshape(nb, nl, heads, n).transpose(1, 3, 0, 2).reshape(nl, n, nb * heads)
    pad = (-(nb * heads)) % LANES
    return jnp.pad(x, ((0, 0), (0, 0), (0, pad))) if pad else x


def _rwkv_branch(zr, prev_s, wkv0_s, p, layer, *, bp, lp, bs, ls, tp, heads, n):
    wr = heads * n
    rk_d = p['w_up'].shape[1]
    ra_d = p['a_up'].shape[1]
    zp, zs = _split_groups(zr, bp, lp, bs, ls)
    sh_p = jnp.concatenate([jnp.zeros((bp, 1, zr.shape[1]), F32), zp[:, :-1]], axis=1)
    sh_s = jnp.concatenate([prev_s[:, None].astype(F32), zs[:, :-1]], axis=1)
    t_real = bp * lp + bs * ls
    shifted = jnp.concatenate([sh_p.reshape(bp * lp, -1), sh_s.reshape(bs * ls, -1),
                               jnp.zeros((tp - t_real, zr.shape[1]), F32)], axis=0)
    zm = zr + (shifted - zr) * p['mu'].astype(F32)
    r = zm[:, :wr]
    v = zm[:, 2 * wr:3 * wr]
    half = math.gcd(wr, 1024)
    hid = jnp.arange(half, dtype=jnp.int32) // n
    e_mat = (hid[:, None] == hid[None, :]).astype(BF16)
    kk, decay, bvec, k2, g = rwkv_prep(zm, p, layer, e_mat, wr=wr)

    def run(lo, nb, nl, s0):
        hl = [_heads_on_lanes(t[lo:lo + nb * nl], nb, nl, heads, n) for t in (kk, decay, bvec, k2, r, v)]
        y, s = wkv7(*hl, s0)
        y = y[:, :, :nb * heads].reshape(nl, n, nb, heads).transpose(2, 0, 3, 1).reshape(nb * nl, wr)
        s = s[:, :, :nb * heads].reshape(n, n, nb, heads).transpose(2, 3, 0, 1)
        return y, s

    nh_p = bp * heads + (-(bp * heads)) % LANES
    y_p, s_p = run(0, bp, lp, jnp.zeros((n, n, nh_p), F32))
    s0_s = wkv0_s.astype(F32).transpose(2, 3, 0, 1).reshape(n, n, bs * heads)
    pad_s = (-(bs * heads)) % LANES
    if pad_s:
        s0_s = jnp.pad(s0_s, ((0, 0), (0, 0), (0, pad_s)))
    y_s, s_s = run(bp * lp, bs, ls, s0_s)
    y = jnp.concatenate([y_p, y_s, jnp.zeros((tp - t_real, wr), F32)], axis=0)
    y = rwkv_post(y, zm, k2, g, p, e_mat, wr=wr, n=n)
    return y, s_p, s_s, zp[:, -1], zs[:, -1]


def kernel(x_prompt, x_sample, state_s5_re, state_s5_im, state_rwkv_shift, state_rwkv_wkv, state_hgrn, meta_tokens, norm_mix_w, w_in, s5_lam_re, s5_lam_im, s5_log_dt, s5_b_re, s5_b_im, s5_c_re, s5_c_im, s5_d, s5_w_glu, s5_b_glu, rwkv_mu, rwkv_w0, rwkv_w_up, rwkv_a0, rwkv_a_up, rwkv_g_up, rwkv_k_k, rwkv_k_a, rwkv_r_k, rwkv_ln_w, rwkv_ln_b, hgrn_lb, hgrn_norm_w, p_s5, p_rwkv, p_hgrn, w_out, norm_ffn_w, router_g_w, router_g_b, router_e_w, router_e_b, moe_w_gate, moe_w_up, moe_w_down, final_norm_w):
    depth, D, _ = w_in.shape
    bp, seq, _ = x_prompt.shape
    bs, ls, _ = x_sample.shape
    n_meta = meta_tokens.shape[0]
    lp = seq + n_meta
    w_s5 = s5_w_glu.shape[1]
    heads_r, n_r = rwkv_r_k.shape[1], rwkv_r_k.shape[2]
    c_rwkv = rwkv_mu.shape[1]
    heads_h, dk_h = state_hgrn.shape[2], state_hgrn.shape[3]
    w_hg = heads_h * dk_h
    t_real = bp * lp + bs * ls
    tile = 1280 if t_real > 4096 else 256
    tp = -(-t_real // tile) * tile

    meta = jnp.broadcast_to(meta_tokens[None].astype(F32), (bp, n_meta, D))
    xp = jnp.concatenate([meta, x_prompt.astype(F32)], axis=1).reshape(bp * lp, D)
    h = jnp.concatenate([xp, x_sample.astype(F32).reshape(bs * ls, D), jnp.zeros((tp - t_real, D), F32)], axis=0)

    sm = jax.nn.softmax(hgrn_lb.astype(F32), axis=0)
    lower = jnp.maximum(jnp.cumsum(sm, axis=0) - sm[0], 0.0)

    outs = {k: [] for k in ('p_re', 'p_im', 'p_shift', 'p_wkv', 'p_hg', 's_re', 's_im', 's_shift', 's_wkv', 's_hg')}
    dims = dict(bp=bp, lp=lp, bs=bs, ls=ls)
    for l in range(depth):
        (n16,) = rms_norm(h, norm_mix_w[l].astype(F32), want32=False, want16=True)
        z = mm(n16, w_in, l)
        s5p = _s5_params(s5_lam_re[l], s5_lam_im[l], s5_log_dt[l], s5_b_re[l], s5_b_im[l],
                         s5_c_re[l], s5_c_im[l], s5_d[l])
        y_s5, pre, pim, sre, sim = _s5_branch(z[:, :w_s5], s5p, state_s5_re[l], state_s5_im[l],
                                              s5_w_glu, s5_b_glu[l], l, tp=tp, **dims)
        rp = dict(mu=rwkv_mu[l], w0=rwkv_w0[l], w_up=rwkv_w_up, a0=rwkv_a0[l], a_up=rwkv_a_up,
                  g_up=rwkv_g_up, k_k=rwkv_k_k[l], k_a=rwkv_k_a[l], r_k=rwkv_r_k[l],
                  ln_w=rwkv_ln_w[l], ln_b=rwkv_ln_b[l])
        y_rw, wkv_p, wkv_s, shift_p, shift_s = _rwkv_branch(
            z[:, w_s5:w_s5 + c_rwkv], state_rwkv_shift[l], state_rwkv_wkv[l], rp, l,
            tp=tp, heads=heads_r, n=n_r, **dims)
        o_p, o_s, hg_p, hg_s = hgrn2(z, w_s5 + c_rwkv, lower[l], hgrn_norm_w[l], state_hgrn[l],
                                     heads=heads_h, dk=dk_h, **dims)
        y_hg = jnp.concatenate([o_p, o_s, jnp.zeros((tp - t_real, w_hg), F32)], axis=0)
        g0 = w_s5 + c_rwkv + 4 * w_hg
        merged = merge_branches(z, g0, y_s5.astype(BF16), y_rw.astype(BF16), y_hg.astype(BF16),
                                p_s5, p_rwkv, p_hgrn, l)
        h = mm(merged, w_out, l, res=h)
        n32, n16 = rms_norm(h, norm_ffn_w[l].astype(F32), want32=True, want16=True)
        h = h + moe_ffn(n32, n16, t_real, router_g_w[l], router_g_b[l], router_e_w[l], router_e_b[l],
                        moe_w_gate, moe_w_up, moe_w_down, l)
        for key, val in (('p_re', pre), ('p_im', pim), ('p_shift', shift_p), ('p_wkv', wkv_p), ('p_hg', hg_p),
                         ('s_re', sre), ('s_im', sim), ('s_shift', shift_s), ('s_wkv', wkv_s), ('s_hg', hg_s)):
            outs[key].append(val)

    (y,) = rms_norm(h, final_norm_w.astype(F32), want32=True, want16=False)
    yp = y[:bp * lp].reshape(bp, lp, D)[:, n_meta:]
    ys = y[bp * lp:t_real].reshape(bs, ls, D)
    st = lambda key: jnp.stack(outs[key], axis=0).astype(F32)
    return (yp, ys, st('p_re'), st('p_im'), st('p_shift'), st('p_wkv'), st('p_hg'),
            st('s_re'), st('s_im'), st('s_shift'), st('s_wkv'), st('s_hg'))
```

```python
import functools
import math

import jax
import jax.numpy as jnp
from jax import lax
from jax.experimental import pallas as pl
from jax.experimental.pallas import tpu as pltpu

F32 = jnp.float32
BF16 = jnp.bfloat16
HI = lax.Precision.HIGHEST

LANES = 128
SUBLANES = 8
VMEM_LIMIT = 56 * 1024 * 1024
RMS_EPS = 1e-6
RWKV_LN_EPS = 64e-5
TOP_K = 2
S5_GROUPS_PER_BLOCK = 8
HGRN_CHUNK = 48
HGRN_SUB = 16
HGRN_SAMPLE_SEQS = 8
MOE_ROWS = 256
MOE_TN = 256


def _cparams(n_axes):
    return pltpu.CompilerParams(dimension_semantics=("arbitrary",) * n_axes,
                                vmem_limit_bytes=VMEM_LIMIT)


def _pick(n, cands):
    for c in cands:
        if n % c == 0:
            return c
    raise ValueError(f"no tile for {n} in {cands}")


def _mm_kernel(*refs, has_res):
    if has_res:
        a_ref, w_ref, r_ref, o_ref, wc_ref = refs
    else:
        a_ref, w_ref, o_ref, wc_ref = refs
    m = pl.program_id(1)
    k = pl.program_id(2)

    @pl.when(m == 0)
    def _():
        wc_ref[k] = w_ref[0].astype(BF16)

    p = jnp.dot(a_ref[...], wc_ref[k], preferred_element_type=F32)

    @pl.when(k == 0)
    def _():
        o_ref[...] = (r_ref[...] + p) if has_res else p

    @pl.when(k > 0)
    def _():
        o_ref[...] += p


def mm(a, w, layer, res=None):
    M, K = a.shape
    N = w.shape[2]
    if K * N >= 4096 * 4096:
        tm = _pick(M, (640, 512, 256, 128))
        tn = _pick(N, (640, 512, 256, 128))
        tk = K
    else:
        tm = _pick(M, (1280, 1024, 512, 256, 128))
        tn = _pick(N, (1280, 1024, 512, 256, 128))
        tk = _pick(K, (1024, 512, 256, 128, K))
    nk = K // tk
    in_specs = [pl.BlockSpec((tm, tk), lambda n, m, k: (m, k)),
                pl.BlockSpec((1, tk, tn), lambda n, m, k: (layer, jnp.where(m == 0, k, nk - 1), n))]
    args = [a, w]
    if res is not None:
        in_specs.append(pl.BlockSpec((tm, tn), lambda n, m, k: (m, n)))
        args.append(res)
    return pl.pallas_call(
        functools.partial(_mm_kernel, has_res=res is not None),
        grid=(N // tn, M // tm, nk),
        in_specs=in_specs,
        out_specs=pl.BlockSpec((tm, tn), lambda n, m, k: (m, n)),
        out_shape=jax.ShapeDtypeStruct((M, N), F32),
        scratch_shapes=[pltpu.VMEM((nk, tk, tn), BF16)],
        compiler_params=_cparams(3),
    )(*args)


def _merge_kernel(ys_ref, yr_ref, yh_ref, ps_ref, pr_ref, ph_ref, g0_ref, g1_ref, g2_ref, o_ref,
                  cs_ref, cr_ref, ch_ref):
    @pl.when(pl.program_id(1) == 0)
    def _():
        cs_ref[...] = ps_ref[0].astype(BF16)
        cr_ref[...] = pr_ref[0].astype(BF16)
        ch_ref[...] = ph_ref[0].astype(BF16)

    acc = (jax.nn.sigmoid(g0_ref[...]) * jnp.dot(ys_ref[...], cs_ref[...], preferred_element_type=F32)
           + jax.nn.sigmoid(g1_ref[...]) * jnp.dot(yr_ref[...], cr_ref[...], preferred_element_type=F32)
           + jax.nn.sigmoid(g2_ref[...]) * jnp.dot(yh_ref[...], ch_ref[...], preferred_element_type=F32))
    o_ref[...] = acc.astype(BF16)


def merge_branches(z, gate_col0, ys, yr, yh, p_s5, p_rwkv, p_hgrn, layer):
    M = z.shape[0]
    D = p_s5.shape[2]
    tm = _pick(M, (1280, 1024, 512, 256, 128))
    tn = _pick(math.gcd(D, gate_col0), (256, 128))
    gb = gate_col0 // tn
    nd = D // tn

    def aspec(y):
        return pl.BlockSpec((tm, y.shape[1]), lambda n, m: (m, 0))

    def wspec(p):
        return pl.BlockSpec((1, p.shape[1], tn), lambda n, m: (layer, 0, n))

    def gspec(i):
        return pl.BlockSpec((tm, tn), lambda n, m: (m, gb + i * nd + n))

    return pl.pallas_call(
        _merge_kernel,
        grid=(nd, M // tm),
        in_specs=[aspec(ys), aspec(yr), aspec(yh), wspec(p_s5), wspec(p_rwkv), wspec(p_hgrn),
                  gspec(0), gspec(1), gspec(2)],
        out_specs=pl.BlockSpec((tm, tn), lambda n, m: (m, n)),
        out_shape=jax.ShapeDtypeStruct((M, D), BF16),
        scratch_shapes=[pltpu.VMEM((p.shape[1], tn), BF16) for p in (p_s5, p_rwkv, p_hgrn)],
        compiler_params=_cparams(2),
    )(ys, yr, yh, p_s5, p_rwkv, p_hgrn, z, z, z)


def _router_kernel(n_ref, w_ref, b_ref, o_ref, *, n_grp, n_exp):
    e_per = n_exp // n_grp
    logits = jnp.dot(n_ref[...], w_ref[...], precision=HI, preferred_element_type=F32) + b_ref[...]
    lane = lax.broadcasted_iota(jnp.int32, logits.shape, 1)
    lane_f = lane.astype(F32)
    big = float(4 * LANES)
    neg = -jnp.inf
    is_g = lane < n_grp
    lg = jnp.where(is_g, logits, neg)
    eg = jnp.exp(lg - jnp.max(lg, axis=-1, keepdims=True))
    pg = eg / jnp.sum(eg, axis=-1, keepdims=True)
    p_g = jnp.max(pg, axis=-1, keepdims=True)
    g_idx = jnp.min(jnp.where(is_g & (pg == p_g), lane_f, big), axis=-1, keepdims=True)
    e_f = lane_f - float(n_grp)
    in_sel = (e_f >= g_idx * e_per) & (e_f < (g_idx + 1.0) * e_per)
    le = jnp.where(in_sel, logits, neg)
    ee = jnp.exp(le - jnp.max(le, axis=-1, keepdims=True))
    pe = ee / jnp.sum(ee, axis=-1, keepdims=True)
    p1 = jnp.max(pe, axis=-1, keepdims=True)
    i1 = jnp.min(jnp.where(in_sel & (pe == p1), e_f, big), axis=-1, keepdims=True)
    rest = in_sel & (e_f != i1)
    p2 = jnp.max(jnp.where(rest, pe, -1.0), axis=-1, keepdims=True)
    i2 = jnp.min(jnp.where(rest & (pe == p2), e_f, big), axis=-1, keepdims=True)
    den = p1 + p2
    out = jnp.where(lane == 0, p_g * p1 / den,
                    jnp.where(lane == 1, p_g * p2 / den,
                              jnp.where(lane == 2, i1, jnp.where(lane == 3, i2, 0.0))))
    o_ref[...] = out


def router(n32, w, b, *, n_grp, n_exp):
    M, K = n32.shape
    tm = _pick(M, (256, 128))
    return pl.pallas_call(
        functools.partial(_router_kernel, n_grp=n_grp, n_exp=n_exp),
        grid=(M // tm,),
        in_specs=[pl.BlockSpec((tm, K), lambda m: (m, 0)),
                  pl.BlockSpec((K, LANES), lambda m: (0, 0)),
                  pl.BlockSpec((1, LANES), lambda m: (0, 0))],
        out_specs=pl.BlockSpec((tm, LANES), lambda m: (m, 0)),
        out_shape=jax.ShapeDtypeStruct((M, LANES), F32),
        compiler_params=_cparams(1),
    )(n32, w, b)


def _rms_kernel(x_ref, w_ref, *o_refs, want32, want16):
    x = x_ref[...]
    y = x * lax.rsqrt(jnp.mean(x * x, axis=-1, keepdims=True) + RMS_EPS) * w_ref[...]
    if want32:
        o_refs[0][...] = y
    if want16:
        o_refs[-1][...] = y.astype(BF16)


def rms_norm(x, w, *, want32, want16):
    M, D = x.shape
    tm = _pick(M, (256, 128))
    dts = ([F32] if want32 else []) + ([BF16] if want16 else [])
    return pl.pallas_call(
        functools.partial(_rms_kernel, want32=want32, want16=want16),
        grid=(M // tm,),
        in_specs=[pl.BlockSpec((tm, D), lambda m: (m, 0)),
                  pl.BlockSpec((1, D), lambda m: (0, 0))],
        out_specs=[pl.BlockSpec((tm, D), lambda m: (m, 0)) for _ in dts],
        out_shape=[jax.ShapeDtypeStruct((M, D), dt) for dt in dts],
        compiler_params=_cparams(1),
    )(x, w.reshape(1, D))


def _gelu_tanh(y):
    return 0.5 * y * (1.0 + jnp.tanh(math.sqrt(2.0 / math.pi) * (y + 0.044715 * (y * y * y))))


def _s5_kernel(u_ref, bm_ref, cm_ref, are_ref, aim_ref, d_ref, x0re_ref, x0im_ref,
               y_ref, xlre_ref, xlim_ref, xs_ref, sre_ref, sim_ref, *, steps, rows, half):
    tc = pl.program_id(2)

    @pl.when(tc == 0)
    def _():
        sre_ref[...] = x0re_ref[...]
        sim_ref[...] = x0im_ref[...]

    u = u_ref[...]
    bm = bm_ref[0]
    u_hi = u.astype(BF16)
    u_lo = (u - u_hi.astype(F32)).astype(BF16)
    b_hi = bm.astype(BF16)
    b_lo = (bm - b_hi.astype(F32)).astype(BF16)
    xs_ref[...] = (jnp.dot(u_hi, b_hi, preferred_element_type=F32)
                   + jnp.dot(u_hi, b_lo, preferred_element_type=F32)
                   + jnp.dot(u_lo, b_hi, preferred_element_type=F32))
    a_re = are_ref[0]
    a_im = aim_ref[0]

    def step(t, carry):
        x_re, x_im = carry
        r0 = pl.multiple_of(t * rows, rows)
        b_re = xs_ref[pl.ds(r0, rows), 0:half]
        b_im = xs_ref[pl.ds(r0, rows), half:2 * half]
        n_re = a_re * x_re - a_im * x_im + b_re
        n_im = a_re * x_im + a_im * x_re + b_im
        xs_ref[pl.ds(r0, rows), 0:half] = n_re
        xs_ref[pl.ds(r0, rows), half:2 * half] = n_im
        return n_re, n_im

    x_re, x_im = lax.fori_loop(0, steps, step, (sre_ref[...], sim_ref[...]), unroll=min(8, steps))
    sre_ref[...] = x_re
    sim_ref[...] = x_im
    xlre_ref[...] = x_re
    xlim_ref[...] = x_im
    y = jnp.dot(xs_ref[...].astype(BF16), cm_ref[0].astype(BF16), preferred_element_type=F32) + d_ref[0] * u
    y_ref[...] = _gelu_tanh(y)


def s5_scan(u2, bm, cm, a_re, a_im, dvec, x0_re, x0_im, *, groups, ntc, steps, rows):
    n_rows, W = u2.shape
    gbn = W // LANES
    half = bm.shape[2] // 2
    blk = steps * rows
    kern = functools.partial(_s5_kernel, steps=steps, rows=rows, half=half)
    return pl.pallas_call(
        kern,
        grid=(gbn, groups, ntc),
        in_specs=[pl.BlockSpec((blk, LANES), lambda g, s, t: (s * ntc + t, g)),
                  pl.BlockSpec((1, LANES, 2 * half), lambda g, s, t: (g, 0, 0)),
                  pl.BlockSpec((1, 2 * half, LANES), lambda g, s, t: (g, 0, 0)),
                  pl.BlockSpec((1, 1, half), lambda g, s, t: (g, 0, 0)),
                  pl.BlockSpec((1, 1, half), lambda g, s, t: (g, 0, 0)),
                  pl.BlockSpec((1, 1, LANES), lambda g, s, t: (g, 0, 0)),
                  pl.BlockSpec((rows, half), lambda g, s, t: (s, g)),
                  pl.BlockSpec((rows, half), lambda g, s, t: (s, g))],
        out_specs=[pl.BlockSpec((blk, LANES), lambda g, s, t: (s * ntc + t, g)),
                   pl.BlockSpec((rows, half), lambda g, s, t: (s, g)),
                   pl.BlockSpec((rows, half), lambda g, s, t: (s, g))],
        out_shape=[jax.ShapeDtypeStruct((n_rows, W), F32),
                   jax.ShapeDtypeStruct(x0_re.shape, F32),
                   jax.ShapeDtypeStruct(x0_im.shape, F32)],
        scratch_shapes=[pltpu.VMEM((blk, 2 * half), F32),
                        pltpu.VMEM((rows, half), F32),
                        pltpu.VMEM((rows, half), F32)],
        compiler_params=_cparams(3),
    )(u2, bm, cm, a_re, a_im, dvec, x0_re, x0_im)


def _s5_params(lam_re, lam_im, log_dt, b_re, b_im, c_re, c_im, d_skip):
    G, P, H = b_re.shape
    gpb = S5_GROUPS_PER_BLOCK
    gbn = G // gpb
    lam = lax.complex(lam_re.astype(F32), lam_im.astype(F32))
    dt = jnp.exp(log_dt.astype(F32))[:, None]
    a_bar = jnp.exp(lam * dt)
    b_bar = ((a_bar - 1.0) / lam)[..., None] * lax.complex(b_re.astype(F32), b_im.astype(F32))
    eye = jnp.eye(gpb, dtype=F32)

    def bdiag_in(m):
        m = m.reshape(gbn, gpb, P, H).transpose(0, 1, 3, 2)
        return jnp.einsum('aghp,gk->aghkp', m, eye).reshape(gbn, gpb * H, gpb * P)

    def bdiag_out(m):
        m = m.reshape(gbn, gpb, H, P)
        return jnp.einsum('aghp,gk->agpkh', m, eye).reshape(gbn, gpb * P, gpb * H)

    bm = jnp.concatenate([bdiag_in(jnp.real(b_bar)), bdiag_in(jnp.imag(b_bar))], axis=2)
    cm = jnp.concatenate([bdiag_out(c_re.astype(F32)), bdiag_out(-c_im.astype(F32))], axis=1)
    a_re = jnp.real(a_bar).reshape(gbn, 1, gpb * P)
    a_im = jnp.imag(a_bar).reshape(gbn, 1, gpb * P)
    dvec = d_skip.astype(F32).reshape(gbn, 1, gpb * H)
    return bm, cm, a_re, a_im, dvec


def _pair_to_heads(x0, x1):
    n = LANES // 2
    at = jnp.concatenate([x0, x1], axis=0).T
    top, bot = at[0:n], at[n:2 * n]
    lo = lax.broadcasted_iota(jnp.int32, top.shape, 1) < n
    return (jnp.where(lo, top, pltpu.roll(bot, n, axis=1)),
            jnp.where(lo, pltpu.roll(top, n, axis=1), bot))


def _pair_from_heads(y0, y1):
    n = LANES // 2
    lo = lax.broadcasted_iota(jnp.int32, y0.shape, 1) < n
    top = jnp.where(lo, y0, pltpu.roll(y1, n, axis=1))
    bot = jnp.where(lo, pltpu.roll(y0, n, axis=1), y1)
    a = jnp.concatenate([top, bot], axis=0).T
    return a[0:n], a[n:2 * n]


def _wkv_kernel(kk_ref, w_ref, b_ref, k_ref, r_ref, v_ref, s0_ref, y_ref, so_ref,
                s_ref, sa_ref, tl_ref, yt_ref, *, steps, n):
    @pl.when(pl.program_id(1) == 0)
    def _():
        s_ref[...] = s0_ref[...]

    ins = (kk_ref, w_ref, b_ref, k_ref, r_ref, v_ref)

    def relayout_in(pi, _):
        t0 = 2 * pi
        for a, ref in enumerate(ins):
            h0, h1 = _pair_to_heads(ref[t0], ref[t0 + 1])
            tl_ref[a, t0] = h0
            tl_ref[a, t0 + 1] = h1
        return 0

    lax.fori_loop(0, steps // 2, relayout_in, 0, unroll=2)

    def tstep(t, _):
        kk = tl_ref[0, t]

        def sa_pass(v, _):
            sa_ref[pl.ds(v, 1), :] = -jnp.sum(s_ref[v] * kk, axis=0, keepdims=True)
            return 0

        lax.fori_loop(0, n, sa_pass, 0, unroll=8)
        w = tl_ref[1, t]
        b = tl_ref[2, t]
        kx = tl_ref[3, t]
        r = tl_ref[4, t]

        def update_pass(v, _):
            sn = s_ref[v] * w + sa_ref[pl.ds(v, 1), :] * b + tl_ref[5, t, pl.ds(v, 1), :] * kx
            s_ref[v] = sn
            yt_ref[t, pl.ds(v, 1), :] = jnp.sum(sn * r, axis=0, keepdims=True)
            return 0

        lax.fori_loop(0, n, update_pass, 0, unroll=8)
        return 0

    lax.fori_loop(0, steps, tstep, 0)

    def relayout_out(pi, _):
        t0 = 2 * pi
        y0, y1 = _pair_from_heads(yt_ref[t0], yt_ref[t0 + 1])
        y_ref[t0] = y0
        y_ref[t0 + 1] = y1
        return 0

    lax.fori_loop(0, steps // 2, relayout_out, 0, unroll=2)
    so_ref[...] = s_ref[...]


def wkv7(seqs, s0):
    NT, R, _ = seqs[0].shape
    n = LANES // 2
    groups = R // n
    steps = _pick(NT, (48, 16, 8, 4, 2))
    seq_spec = pl.BlockSpec((steps, n, LANES), lambda g, t: (t, g, 0))
    st_spec = pl.BlockSpec((n, n, LANES), lambda g, t: (0, 0, g))
    return pl.pallas_call(
        functools.partial(_wkv_kernel, steps=steps, n=n),
        grid=(groups, NT // steps),
        in_specs=[seq_spec] * 6 + [st_spec],
        out_specs=[seq_spec, st_spec],
        out_shape=[jax.ShapeDtypeStruct((NT, R, LANES), F32), jax.ShapeDtypeStruct((n, n, groups * LANES), F32)],
        scratch_shapes=[pltpu.VMEM((n, n, LANES), F32), pltpu.VMEM((n, LANES), F32),
                        pltpu.VMEM((6, steps, n, LANES), F32), pltpu.VMEM((steps, n, LANES), F32)],
        compiler_params=_cparams(2),
    )(*seqs, s0)


def _head_sum(x, e_ref):
    hi = x.astype(BF16)
    lo = (x - hi.astype(F32)).astype(BF16)
    e = e_ref[...]
    return jnp.dot(hi, e, preferred_element_type=F32) + jnp.dot(lo, e, preferred_element_type=F32)


def _rwkv_prep_kernel(r_ref, k_ref, v_ref, xw_ref, xa_ref, xg0_ref, xg1_ref, wup_ref, aup_ref, gup_ref,
                      w0_ref, a0_ref, kk_w_ref, ka_ref, e_ref,
                      kk_o, w_o, b_o, k_o, r_o, v_o, g_o):
    r_o[...] = r_ref[...]
    v_o[...] = v_ref[...]
    k = k_ref[...]
    lw = w0_ref[...] + jnp.dot(jnp.tanh(xw_ref[...]).astype(BF16), wup_ref[0].astype(BF16),
                               preferred_element_type=F32)
    w_log = -(jnp.maximum(-lw, 0.0) + jnp.log1p(jnp.exp(-jnp.abs(lw)))) - 0.5
    w_o[...] = jnp.exp(-jnp.exp(w_log))
    a = jax.nn.sigmoid(a0_ref[...] + jnp.dot(xa_ref[...].astype(BF16), aup_ref[0].astype(BF16),
                                             preferred_element_type=F32))
    xg = jnp.concatenate([xg0_ref[...], xg1_ref[...]], axis=1)
    g_o[...] = jnp.dot(jax.nn.sigmoid(xg).astype(BF16), gup_ref[0].astype(BF16), preferred_element_type=F32)
    kx = k * kk_w_ref[...]
    kk = kx / jnp.maximum(jnp.sqrt(_head_sum(kx * kx, e_ref)), 1e-12)
    kk_o[...] = kk
    b_o[...] = kk * a
    k_o[...] = k * (1.0 + (a - 1.0) * ka_ref[...])


def _seq_specs(tm, half, nh, nt, time_major):
    if time_major:
        return pl.BlockSpec((tm, half), lambda b, i, hf: (i, b * nh + hf)), lambda nb: (nt * tm, nb * nh * half)
    return pl.BlockSpec((tm, half), lambda b, i, hf: (b * nt + i, hf)), lambda nb: (nb * nt * tm, nh * half)


def rwkv_prep(zm, p, layer, e_mat, *, wr, row_blk0, nb, nt, tm, time_major):
    half = e_mat.shape[0]
    nh = wr // half
    rk_d = p['w_up'].shape[1]
    ra_d = p['a_up'].shape[1]
    rg_d = p['g_up'].shape[1]
    assert rk_d == LANES and ra_d == LANES and rg_d % (2 * LANES) == 0 and wr % half == 0
    xw_blk = 3 * wr // LANES
    xg_blk = (3 * wr + rk_d + ra_d) // (rg_d // 2)
    assert (3 * wr + rk_d + ra_d) % (rg_d // 2) == 0

    def zspec(width, col):
        return pl.BlockSpec((tm, width), lambda b, i, hf: (row_blk0 + b * nt + i, col(hf)))

    def vec(x):
        return x.astype(F32).reshape(1, wr)

    vspec = pl.BlockSpec((1, half), lambda b, i, hf: (0, hf))
    seq_spec, seq_shape = _seq_specs(tm, half, nh, nt, time_major)
    g_spec, g_shape = _seq_specs(tm, half, nh, nt, False)
    return pl.pallas_call(
        _rwkv_prep_kernel,
        grid=(nb, nt, nh),
        in_specs=[zspec(half, lambda hf: hf), zspec(half, lambda hf: nh + hf), zspec(half, lambda hf: 2 * nh + hf),
                  zspec(LANES, lambda hf: xw_blk), zspec(LANES, lambda hf: xw_blk + 1),
                  zspec(rg_d // 2, lambda hf: xg_blk), zspec(rg_d // 2, lambda hf: xg_blk + 1),
                  pl.BlockSpec((1, rk_d, half), lambda b, i, hf: (layer, 0, hf)),
                  pl.BlockSpec((1, ra_d, half), lambda b, i, hf: (layer, 0, hf)),
                  pl.BlockSpec((1, rg_d, half), lambda b, i, hf: (layer, 0, hf)),
                  vspec, vspec, vspec, vspec,
                  pl.BlockSpec((half, half), lambda b, i, hf: (0, 0))],
        out_specs=[seq_spec] * 6 + [g_spec],
        out_shape=[jax.ShapeDtypeStruct(seq_shape(nb), F32)] * 6 + [jax.ShapeDtypeStruct(g_shape(nb), F32)],
        compiler_params=_cparams(3),
    )(zm, zm, zm, zm, zm, zm, zm, p['w_up'], p['a_up'], p['g_up'],
      vec(p['w0']), vec(p['a0']), vec(p['k_k']), vec(p['k_a']), e_mat)


def _rwkv_post_kernel(y_ref, r_ref, k_ref, v_ref, g_ref, lnw_ref, lnb_ref, rk_ref, e_ref, o_ref, *, n):
    y = y_ref[...]
    d = y - _head_sum(y, e_ref) * (1.0 / n)
    var = _head_sum(d * d, e_ref) * (1.0 / n)
    yn = d * lax.rsqrt(var + RWKV_LN_EPS) * lnw_ref[...] + lnb_ref[...]
    bonus = _head_sum(r_ref[...] * k_ref[...] * rk_ref[...], e_ref) * v_ref[...]
    o_ref[...] = ((yn + bonus) * g_ref[...]).astype(BF16)


def rwkv_post(y, zm, k2, g, p, e_mat, *, wr, n, row_blk0, nb, nt, tm, time_major):
    half = e_mat.shape[0]
    nh = wr // half
    seq_spec, _ = _seq_specs(tm, half, nh, nt, time_major)
    bm_spec, bm_shape = _seq_specs(tm, half, nh, nt, False)
    vspec = pl.BlockSpec((1, half), lambda b, i, hf: (0, hf))

    def zspec(sec):
        return pl.BlockSpec((tm, half), lambda b, i, hf: (row_blk0 + b * nt + i, sec * nh + hf))

    def vec(x):
        return x.astype(F32).reshape(1, wr)

    return pl.pallas_call(
        functools.partial(_rwkv_post_kernel, n=n),
        grid=(nb, nt, nh),
        in_specs=[seq_spec, zspec(0), seq_spec, zspec(2), bm_spec, vspec, vspec, vspec,
                  pl.BlockSpec((half, half), lambda b, i, hf: (0, 0))],
        out_specs=bm_spec,
        out_shape=jax.ShapeDtypeStruct(bm_shape(nb), BF16),
        compiler_params=_cparams(3),
    )(y, zm, k2, zm, g, vec(p['ln_w']), vec(p['ln_b']), vec(p['r_k']), e_mat)


def _hgrn_gates(zq, zf, loglb, log1mlb, onemlb):
    q = zq * jax.nn.sigmoid(zq)
    log_sig = jnp.minimum(zf, 0.0) - jnp.log1p(jnp.exp(-jnp.abs(zf)))
    t = log1mlb + log_sig
    hi = jnp.maximum(loglb, t)
    log_f = hi + jnp.log1p(jnp.exp(-jnp.abs(loglb - t)))
    k = onemlb * jax.nn.sigmoid(-zf)
    return q, log_f, k


def _hgrn_out(o, zg, nw):
    o = o * lax.rsqrt(jnp.mean(o * o, axis=-1, keepdims=True) + RMS_EPS)
    return o * nw * (zg * jax.nn.sigmoid(zg))


def _col_bcast(row, n_lanes):
    e = jnp.concatenate([row, jnp.zeros((SUBLANES - 1, row.shape[1]), F32)], axis=0)
    ones = jnp.concatenate([jnp.ones((1, n_lanes), F32), jnp.zeros((SUBLANES - 1, n_lanes), F32)], axis=0)
    return lax.dot_general(e, ones, (((0,), (0,)), ((), ())), precision=HI, preferred_element_type=F32)


def _hgrn_prompt_kernel(*refs, rows, sub, heads):
    z_refs = refs[:4 * heads]
    loglb_ref, log1mlb_ref, onemlb_ref, nw_ref, o_ref, so_ref, s_ref = refs[4 * heads:]

    @pl.when(pl.program_id(1) == 0)
    def _():
        s_ref[...] = jnp.zeros_like(s_ref)

    dk = s_ref.shape[1]
    for h in range(heads):
        zq_ref, zf_ref, zi_ref, zg_ref = (z_refs[sec * heads + h] for sec in range(4))
        o, s_new = _hgrn_chunk(zq_ref[...], zf_ref[...], zi_ref[...], loglb_ref[h], log1mlb_ref[h],
                               onemlb_ref[h], s_ref[h], rows=rows, sub=sub)
        s_ref[h] = s_new
        so_ref[0, h] = s_new
        o_ref[:, h * dk:(h + 1) * dk] = _hgrn_out(o, zg_ref[...], nw_ref[h])


_NN = (((1,), (0,)), ((), ()))
_NT = (((1,), (1,)), ((), ()))
_TN = (((0,), (0,)), ((), ()))


def _dg3(a, b, dims):
    a_hi = a.astype(BF16)
    a_lo = (a - a_hi.astype(F32)).astype(BF16)
    b_hi = b.astype(BF16)
    b_lo = (b - b_hi.astype(F32)).astype(BF16)

    def f(x, y):
        return lax.dot_general(x, y, dims, preferred_element_type=F32)

    return f(a_hi, b_hi) + f(a_hi, b_lo) + f(a_lo, b_hi)


def _hgrn_chunk(zq, zf, v, loglb, log1mlb, onemlb, s_old, *, rows, sub):
    q, log_f, k = _hgrn_gates(zq, zf, loglb, log1mlb, onemlb)
    dv = v.shape[1]
    ri = lax.broadcasted_iota(jnp.int32, (rows, rows), 0)
    ci = lax.broadcasted_iota(jnp.int32, (rows, rows), 1)
    tri = (ci <= ri).astype(F32)
    bcum = jnp.dot(tri, log_f, precision=HI, preferred_element_type=F32)
    o_state = _dg3(q * jnp.exp(bcum), s_old, _NN)
    o_parts = []
    for i in range(rows // sub):
        lo, hi = i * sub, (i + 1) * sub
        b_i = bcum[lo:hi]
        q_i = q[lo:hi]
        o_i = o_state[lo:hi]
        if i > 0:
            b_ref_row = bcum[lo - 1:lo]
            qe = q_i * jnp.exp(b_i - b_ref_row)
            ke = k[0:lo] * jnp.exp(b_ref_row - bcum[0:lo])
            o_i = o_i + _dg3(_dg3(qe, ke, _NT), v[0:lo], _NN)
        t_idx = lax.broadcasted_iota(jnp.int32, (sub, 1), 0)
        for s in range(sub):
            e_s = jnp.exp(jnp.minimum(b_i - b_i[s:s + 1], 0.0))
            col = jnp.sum(q_i * k[lo + s:lo + s + 1] * e_s, axis=-1, keepdims=True)
            col = jnp.where(t_idx >= s, col, 0.0)
            o_i = o_i + col * v[lo + s:lo + s + 1]
        o_parts.append(o_i)
    o = jnp.concatenate(o_parts, axis=0)
    b_end = bcum[rows - 1:rows]
    ke_end = k * jnp.exp(b_end - bcum)
    s_new = _col_bcast(jnp.exp(b_end), dv) * s_old + _dg3(ke_end, v, _TN)
    return o, s_new


def _hgrn_sample_kernel(zq_ref, zf_ref, zi_ref, zg_ref, loglb_ref, log1mlb_ref, onemlb_ref, nw_ref, s0_ref,
                        o_ref, so_ref, *, nseq, slen):
    rows = nseq * slen
    q, log_f, k = _hgrn_gates(zq_ref[...], zf_ref[...], loglb_ref[0], log1mlb_ref[0], onemlb_ref[0])
    v = zi_ref[...]
    dv = v.shape[1]
    ri = lax.broadcasted_iota(jnp.int32, (rows, rows), 0)
    ci = lax.broadcasted_iota(jnp.int32, (rows, rows), 1)
    same = (ri // slen) == (ci // slen)
    tri = ((ci <= ri) & same).astype(F32)
    bcum = jnp.dot(tri, log_f, precision=HI, preferred_element_type=F32)
    t_idx = lax.broadcasted_iota(jnp.int32, (rows, 1), 0)
    o = jnp.zeros((rows, dv), F32)
    for s in range(rows):
        e_s = jnp.exp(jnp.minimum(bcum - bcum[s:s + 1], 0.0))
        col = jnp.sum(q * k[s:s + 1] * e_s, axis=-1, keepdims=True)
        col = jnp.where((t_idx >= s) & (t_idx < (s // slen + 1) * slen), col, 0.0)
        o = o + col * v[s:s + 1]
    qe = q * jnp.exp(bcum)
    for j in range(nseq):
        in_seq = (t_idx >= j * slen) & (t_idx < (j + 1) * slen)
        s_old = s0_ref[j, 0]
        o = o + _dg3(jnp.where(in_seq, qe, 0.0), s_old, _NN)
        b_end = bcum[(j + 1) * slen - 1:(j + 1) * slen]
        ke = jnp.where(in_seq, k * jnp.exp(b_end - bcum), 0.0)
        so_ref[j, 0] = _col_bcast(jnp.exp(b_end), dv) * s_old + _dg3(ke, v, _TN)
    o_ref[...] = _hgrn_out(o, zg_ref[...], nw_ref[0])


def hgrn2(z, col0, lb, norm_w, s0_sample, *, bp, lp, bs, ls, heads, dk):
    assert dk == LANES and col0 % LANES == 0
    cb = col0 // LANES
    loglb = jnp.log(lb).reshape(heads, 1, dk)
    log1mlb = jnp.log1p(-lb).reshape(heads, 1, dk)
    onemlb = (1.0 - lb).reshape(heads, 1, dk)
    nw = norm_w.astype(F32).reshape(heads, 1, dk)
    W = heads * dk
    par_full = pl.BlockSpec((heads, 1, dk), lambda b, c: (0, 0, 0))

    rc = HGRN_CHUNK
    assert lp % rc == 0
    nch = lp // rc

    def zspec(col):
        return pl.BlockSpec((rc, dk), lambda b, c: (b * nch + c, cb + col))

    o_p, s_p = pl.pallas_call(
        functools.partial(_hgrn_prompt_kernel, rows=rc, sub=HGRN_SUB, heads=heads),
        grid=(bp, nch),
        in_specs=[zspec(col) for col in range(4 * heads)] + [par_full] * 4,
        out_specs=[pl.BlockSpec((rc, W), lambda b, c: (b * nch + c, 0)),
                   pl.BlockSpec((1, heads, dk, dk), lambda b, c: (b, 0, 0, 0))],
        out_shape=[jax.ShapeDtypeStruct((bp * lp, W), F32),
                   jax.ShapeDtypeStruct((bp, heads, dk, dk), F32)],
        scratch_shapes=[pltpu.VMEM((heads, dk, dk), F32)],
        compiler_params=_cparams(2),
    )(*([z] * (4 * heads)), loglb, log1mlb, onemlb, nw)

    nseq = HGRN_SAMPLE_SEQS
    rs = nseq * ls
    assert bs % nseq == 0 and rs % SUBLANES == 0 and (bp * lp) % rs == 0
    r0 = bp * lp // rs
    par_spec2 = pl.BlockSpec((1, 1, dk), lambda p, h: (h, 0, 0))

    def zspec_s(sec):
        return pl.BlockSpec((rs, dk), lambda p, h: (r0 + p, cb + sec * heads + h))

    st_spec = pl.BlockSpec((nseq, 1, dk, dk), lambda p, h: (p, h, 0, 0))
    o_s, s_s = pl.pallas_call(
        functools.partial(_hgrn_sample_kernel, nseq=nseq, slen=ls),
        grid=(bs // nseq, heads),
        in_specs=[zspec_s(0), zspec_s(1), zspec_s(2), zspec_s(3), par_spec2, par_spec2, par_spec2, par_spec2,
                  st_spec],
        out_specs=[pl.BlockSpec((rs, dk), lambda p, h: (p, h)), st_spec],
        out_shape=[jax.ShapeDtypeStruct((bs * ls, W), F32),
                   jax.ShapeDtypeStruct((bs, heads, dk, dk), F32)],
        compiler_params=_cparams(2),
    )(z, z, z, z, loglb, log1mlb, onemlb, nw, s0_sample.astype(F32))
    return o_p, o_s, s_p, s_s


def _moe_up_kernel(be_ref, nb_ref, x_ref, wg_ref, wu_ref, h_ref, wgc_ref, wuc_ref):
    i = pl.program_id(1)
    valid = i < nb_ref[0]
    prev = be_ref[jnp.maximum(i - 1, 0)]
    fresh = valid & ((i == 0) | (be_ref[i] != prev))

    @pl.when(fresh)
    def _():
        wgc_ref[...] = wg_ref[0, 0].astype(BF16)
        wuc_ref[...] = wu_ref[0, 0].astype(BF16)

    @pl.when(valid)
    def _():
        x = x_ref[...]
        g = jnp.dot(x, wgc_ref[...], preferred_element_type=F32)
        u = jnp.dot(x, wuc_ref[...], preferred_element_type=F32)
        h_ref[...] = (g * jax.nn.sigmoid(g) * u).astype(BF16)

    @pl.when(jnp.logical_not(valid))
    def _():
        h_ref[...] = jnp.zeros_like(h_ref)


def _moe_down_kernel(be_ref, nb_ref, h_ref, wd_ref, y_ref, wdc_ref):
    i = pl.program_id(1)
    valid = i < nb_ref[0]
    prev = be_ref[jnp.maximum(i - 1, 0)]
    fresh = valid & ((i == 0) | (be_ref[i] != prev))

    @pl.when(fresh)
    def _():
        wdc_ref[...] = wd_ref[0, 0].astype(BF16)

    @pl.when(valid)
    def _():
        y_ref[...] = jnp.dot(h_ref[...], wdc_ref[...], preferred_element_type=F32)

    @pl.when(jnp.logical_not(valid))
    def _():
        y_ref[...] = jnp.zeros_like(y_ref)


def moe_experts(xs, blk_expert, n_used, w_gate, w_up, w_down, layer):
    R, D = xs.shape
    DE = w_gate.shape[3]
    nblk = R // MOE_ROWS
    tn = _pick(DE, (MOE_TN, 128))
    hid = pl.pallas_call(
        _moe_up_kernel,
        grid_spec=pltpu.PrefetchScalarGridSpec(
            num_scalar_prefetch=2,
            grid=(DE // tn, nblk),
            in_specs=[pl.BlockSpec((MOE_ROWS, D), lambda j, i, be, nb: (jnp.minimum(i, nb[0] - 1), 0)),
                      pl.BlockSpec((1, 1, D, tn), lambda j, i, be, nb: (layer, be[i], 0, j)),
                      pl.BlockSpec((1, 1, D, tn), lambda j, i, be, nb: (layer, be[i], 0, j))],
            out_specs=pl.BlockSpec((MOE_ROWS, tn), lambda j, i, be, nb: (i, j)),
            scratch_shapes=[pltpu.VMEM((D, tn), BF16), pltpu.VMEM((D, tn), BF16)]),
        out_shape=jax.ShapeDtypeStruct((R, DE), BF16),
        compiler_params=_cparams(2),
    )(blk_expert, n_used, xs, w_gate, w_up)
    tn2 = _pick(D, (1024, 512, 256, 128))
    return pl.pallas_call(
        _moe_down_kernel,
        grid_spec=pltpu.PrefetchScalarGridSpec(
            num_scalar_prefetch=2,
            grid=(D // tn2, nblk),
            in_specs=[pl.BlockSpec((MOE_ROWS, DE), lambda j, i, be, nb: (jnp.minimum(i, nb[0] - 1), 0)),
                      pl.BlockSpec((1, 1, DE, tn2), lambda j, i, be, nb: (layer, be[i], 0, j))],
            out_specs=pl.BlockSpec((MOE_ROWS, tn2), lambda j, i, be, nb: (i, j)),
            scratch_shapes=[pltpu.VMEM((DE, tn2), BF16)]),
        out_shape=jax.ShapeDtypeStruct((R, D), F32),
        compiler_params=_cparams(2),
    )(blk_expert, n_used, hid, w_down)


def moe_ffn(n32, n16, t_real, rg_w, rg_b, re_w, re_b, w_gate, w_up, w_down, layer):
    Tp, D = n32.shape
    n_grp = rg_w.shape[1]
    n_exp = re_w.shape[1]
    wr = jnp.concatenate([rg_w, re_w], axis=1).astype(F32)
    wr = jnp.pad(wr, ((0, 0), (0, LANES - wr.shape[1])))
    br = jnp.concatenate([rg_b, re_b]).astype(F32)
    br = jnp.pad(br, (0, LANES - br.shape[0])).reshape(1, LANES)
    routed = router(n32, wr, br, n_grp=n_grp, n_exp=n_exp)[:t_real]
    gate = routed[:, 0:TOP_K]
    expert = routed[:, TOP_K:2 * TOP_K].astype(jnp.int32)

    tk = t_real * TOP_K
    flat_e = expert.reshape(tk).astype(jnp.int32)
    order = jnp.argsort(flat_e)
    e_sorted = flat_e[order]
    counts = jnp.bincount(flat_e, length=n_exp)
    padded = (counts + MOE_ROWS - 1) // MOE_ROWS * MOE_ROWS
    pad_end = jnp.cumsum(padded)
    pad_start = pad_end - padded
    start = jnp.cumsum(counts) - counts
    dest = (pad_start[e_sorted] + jnp.arange(tk, dtype=jnp.int32) - start[e_sorted]).astype(jnp.int32)
    n_blocks = -(-tk // MOE_ROWS) + n_exp
    R = n_blocks * MOE_ROWS
    row_tok = jnp.zeros((R,), jnp.int32).at[dest].set((order // TOP_K).astype(jnp.int32))
    n_used = (pad_end[-1] // MOE_ROWS).astype(jnp.int32)
    blk_start = jnp.arange(n_blocks, dtype=jnp.int32) * MOE_ROWS
    blk_start = jnp.minimum(blk_start, (n_used - 1) * MOE_ROWS)
    blk_expert = jnp.minimum(jnp.searchsorted(pad_end, blk_start, side='right'), n_exp - 1).astype(jnp.int32)
    xs = n16[row_tok]
    yb = moe_experts(xs, blk_expert, n_used.reshape(1), w_gate, w_up, w_down, layer)
    slot = jnp.zeros((tk,), jnp.int32).at[order].set(dest).reshape(t_real, TOP_K)
    y = yb[slot[:, 0]] * gate[:, 0:1] + yb[slot[:, 1]] * gate[:, 1:2]
    return jnp.pad(y, ((0, Tp - t_real), (0, 0)))


def _split_groups(x, bp, lp, bs, ls):
    n = x.shape[-1]
    xp = x[:bp * lp].reshape(bp, lp, n)
    xs = x[bp * lp:bp * lp + bs * ls].reshape(bs, ls, n)
    return xp, xs


def _s5_branch(u, prm, x0re_s, x0im_s, w_glu, b_glu, layer, *, bp, lp, bs, ls, tp):
    W = u.shape[1]
    bm, cm, a_re, a_im, dvec = prm
    n_state = x0re_s.shape[1] * x0re_s.shape[2]
    up, us = _split_groups(u, bp, lp, bs, ls)
    rows_p = SUBLANES
    steps_p = _pick(lp, (344, 48, 16, 8))
    up_tm = jnp.pad(up, ((0, rows_p - bp), (0, 0), (0, 0))).transpose(1, 0, 2).reshape(lp * rows_p, W)
    zero = jnp.zeros((rows_p, n_state), F32)
    yp, pre, pim = s5_scan(up_tm, bm, cm, a_re, a_im, dvec, zero, zero,
                           groups=1, ntc=lp // steps_p, steps=steps_p, rows=rows_p)
    yp = yp.reshape(lp, rows_p, W).transpose(1, 0, 2)[:bp].reshape(bp * lp, W)
    rows_s = _pick(bs, (32, 16, 8))
    ng = bs // rows_s
    us_tm = us.reshape(ng, rows_s, ls, W).transpose(0, 2, 1, 3).reshape(bs * ls, W)
    ys, sre, sim = s5_scan(us_tm, bm, cm, a_re, a_im, dvec,
                           x0re_s.reshape(bs, n_state).astype(F32), x0im_s.reshape(bs, n_state).astype(F32),
                           groups=ng, ntc=1, steps=ls, rows=rows_s)
    ys = ys.reshape(ng, ls, rows_s, W).transpose(0, 2, 1, 3).reshape(bs * ls, W)
    y = jnp.concatenate([yp, ys, jnp.zeros((tp - bp * lp - bs * ls, W), F32)], axis=0)
    gl = mm(y.astype(BF16), w_glu, layer)
    y = y * jax.nn.sigmoid(gl + b_glu.astype(F32))
    gshape = x0re_s.shape[1:]
    return (y, pre[:bp].reshape((bp,) + gshape), pim[:bp].reshape((bp,) + gshape),
            sre.reshape((bs,) + gshape), sim.reshape((bs,) + gshape))


def _wkv_state_in(s, heads, n):
    nb = s.shape[0]
    per = LANES // heads
    s = s.astype(F32).reshape(nb // per, per, heads // 2, 2, n, n)
    return s.transpose(4, 5, 0, 3, 1, 2).reshape(n, n, nb * heads)


def _wkv_state_out(s, nb, heads, n):
    per = LANES // heads
    s = s.reshape(n, n, nb // per, 2, per, heads // 2)
    return s.transpose(2, 4, 5, 3, 0, 1).reshape(nb, heads, n, n)


def _rwkv_branch(zr, prev_s, wkv0_s, p, layer, *, bp, lp, bs, ls, tp, heads, n):
    wr = heads * n
    assert n == LANES // 2 and LANES % heads == 0 and bp % (LANES // heads) == 0 and bs % (LANES // heads) == 0
    zp, zs = _split_groups(zr, bp, lp, bs, ls)
    sh_p = jnp.concatenate([jnp.zeros((bp, 1, zr.shape[1]), F32), zp[:, :-1]], axis=1)
    sh_s = jnp.concatenate([prev_s[:, None].astype(F32), zs[:, :-1]], axis=1)
    t_real = bp * lp + bs * ls
    shifted = jnp.concatenate([sh_p.reshape(bp * lp, -1), sh_s.reshape(bs * ls, -1),
                               jnp.zeros((tp - t_real, zr.shape[1]), F32)], axis=0)
    zm = zr + (shifted - zr) * p['mu'].astype(F32)
    half = math.gcd(wr, 1024)
    hid = jnp.arange(half, dtype=jnp.int32) // n
    e_mat = (hid[:, None] == hid[None, :]).astype(BF16)

    tm_p = _pick(lp, (344, 96, 48, 16, 8))
    geo_p = dict(wr=wr, row_blk0=0, nb=bp, nt=lp // tm_p, tm=tm_p, time_major=True)
    *seq_p, g_p = rwkv_prep(zm, p, layer, e_mat, **geo_p)
    y_p, s_p = wkv7([t.reshape(lp, bp * wr // LANES, LANES) for t in seq_p], jnp.zeros((n, n, bp * heads), F32))
    y_p = rwkv_post(y_p.reshape(lp, bp * wr), zm, seq_p[3], g_p, p, e_mat, n=n, **geo_p)

    tm_s = _pick(math.gcd(bp * lp, bs * ls), (256, 128, 64, 32, 16, 8))
    geo_s = dict(wr=wr, row_blk0=bp * lp // tm_s, nb=1, nt=bs * ls // tm_s, tm=tm_s, time_major=False)
    *seq_s, g_s = rwkv_prep(zm, p, layer, e_mat, **geo_s)
    seq_tm = [t.reshape(bs, ls, wr).transpose(1, 0, 2).reshape(ls, bs * wr // LANES, LANES) for t in seq_s]
    y_s, s_s = wkv7(seq_tm, _wkv_state_in(wkv0_s, heads, n))
    y_s = y_s.reshape(ls, bs, wr).transpose(1, 0, 2).reshape(bs * ls, wr)
    y_s = rwkv_post(y_s, zm, seq_s[3], g_s, p, e_mat, n=n, **geo_s)

    y = jnp.concatenate([y_p, y_s, jnp.zeros((tp - t_real, wr), BF16)], axis=0)
    return (y, _wkv_state_out(s_p, bp, heads, n), _wkv_state_out(s_s, bs, heads, n), zp[:, -1], zs[:, -1])


def kernel(x_prompt, x_sample, state_s5_re, state_s5_im, state_rwkv_shift, state_rwkv_wkv, state_hgrn, meta_tokens, norm_mix_w, w_in, s5_lam_re, s5_lam_im, s5_log_dt, s5_b_re, s5_b_im, s5_c_re, s5_c_im, s5_d, s5_w_glu, s5_b_glu, rwkv_mu, rwkv_w0, rwkv_w_up, rwkv_a0, rwkv_a_up, rwkv_g_up, rwkv_k_k, rwkv_k_a, rwkv_r_k, rwkv_ln_w, rwkv_ln_b, hgrn_lb, hgrn_norm_w, p_s5, p_rwkv, p_hgrn, w_out, norm_ffn_w, router_g_w, router_g_b, router_e_w, router_e_b, moe_w_gate, moe_w_up, moe_w_down, final_norm_w):
    depth, D, _ = w_in.shape
    bp, seq, _ = x_prompt.shape
    bs, ls, _ = x_sample.shape
    n_meta = meta_tokens.shape[0]
    lp = seq + n_meta
    w_s5 = s5_w_glu.shape[1]
    heads_r, n_r = rwkv_r_k.shape[1], rwkv_r_k.shape[2]
    c_rwkv = rwkv_mu.shape[1]
    heads_h, dk_h = state_hgrn.shape[2], state_hgrn.shape[3]
    w_hg = heads_h * dk_h
    t_real = bp * lp + bs * ls
    tile = 1280 if t_real > 4096 else 256
    tp = -(-t_real // tile) * tile

    meta = jnp.broadcast_to(meta_tokens[None].astype(F32), (bp, n_meta, D))
    xp = jnp.concatenate([meta, x_prompt.astype(F32)], axis=1).reshape(bp * lp, D)
    h = jnp.concatenate([xp, x_sample.astype(F32).reshape(bs * ls, D), jnp.zeros((tp - t_real, D), F32)], axis=0)

    sm = jax.nn.softmax(hgrn_lb.astype(F32), axis=0)
    lower = jnp.maximum(jnp.cumsum(sm, axis=0) - sm[0], 0.0)

    outs = {k: [] for k in ('p_re', 'p_im', 'p_shift', 'p_wkv', 'p_hg', 's_re', 's_im', 's_shift', 's_wkv', 's_hg')}
    dims = dict(bp=bp, lp=lp, bs=bs, ls=ls)
    for l in range(depth):
        (n16,) = rms_norm(h, norm_mix_w[l].astype(F32), want32=False, want16=True)
        z = mm(n16, w_in, l)
        s5p = _s5_params(s5_lam_re[l], s5_lam_im[l], s5_log_dt[l], s5_b_re[l], s5_b_im[l],
                         s5_c_re[l], s5_c_im[l], s5_d[l])
        y_s5, pre, pim, sre, sim = _s5_branch(z[:, :w_s5], s5p, state_s5_re[l], state_s5_im[l],
                                              s5_w_glu, s5_b_glu[l], l, tp=tp, **dims)
        rp = dict(mu=rwkv_mu[l], w0=rwkv_w0[l], w_up=rwkv_w_up, a0=rwkv_a0[l], a_up=rwkv_a_up,
                  g_up=rwkv_g_up, k_k=rwkv_k_k[l], k_a=rwkv_k_a[l], r_k=rwkv_r_k[l],
                  ln_w=rwkv_ln_w[l], ln_b=rwkv_ln_b[l])
        y_rw, wkv_p, wkv_s, shift_p, shift_s = _rwkv_branch(
            z[:, w_s5:w_s5 + c_rwkv], state_rwkv_shift[l], state_rwkv_wkv[l], rp, l,
            tp=tp, heads=heads_r, n=n_r, **dims)
        o_p, o_s, hg_p, hg_s = hgrn2(z, w_s5 + c_rwkv, lower[l], hgrn_norm_w[l], state_hgrn[l],
                                     heads=heads_h, dk=dk_h, **dims)
        y_hg = jnp.concatenate([o_p, o_s, jnp.zeros((tp - t_real, w_hg), F32)], axis=0)
        g0 = w_s5 + c_rwkv + 4 * w_hg
        merged = merge_branches(z, g0, y_s5.astype(BF16), y_rw.astype(BF16), y_hg.astype(BF16),
                                p_s5, p_rwkv, p_hgrn, l)
        h = mm(merged, w_out, l, res=h)
        n32, n16 = rms_norm(h, norm_ffn_w[l].astype(F32), want32=True, want16=True)
        h = h + moe_ffn(n32, n16, t_real, router_g_w[l], router_g_b[l], router_e_w[l], router_e_b[l],
                        moe_w_gate, moe_w_up, moe_w_down, l)
        for key, val in (('p_re', pre), ('p_im', pim), ('p_shift', shift_p), ('p_wkv', wkv_p), ('p_hg', hg_p),
                         ('s_re', sre), ('s_im', sim), ('s_shift', shift_s), ('s_wkv', wkv_s), ('s_hg', hg_s)):
            outs[key].append(val)

    (y,) = rms_norm(h, final_norm_w.astype(F32), want32=True, want16=False)
    yp = y[:bp * lp].reshape(bp, lp, D)[:, n_meta:]
    ys = y[bp * lp:t_real].reshape(bs, ls, D)
    st = lambda key: jnp.stack(outs[key], axis=0).astype(F32)
    return (yp, ys, st('p_re'), st('p_im'), st('p_shift'), st('p_wkv'), st('p_hg'),
            st('s_re'), st('s_im'), st('s_shift'), st('s_wkv'), st('s_hg'))
```

```python
import functools
import math

import jax
import jax.numpy as jnp
from jax import lax
from jax.experimental import pallas as pl
from jax.experimental.pallas import tpu as pltpu

F32 = jnp.float32
BF16 = jnp.bfloat16
HI = lax.Precision.HIGHEST

LANES = 128
SUBLANES = 8
VMEM_LIMIT = 56 * 1024 * 1024
RMS_EPS = 1e-6
RWKV_LN_EPS = 64e-5
TOP_K = 2
S5_GROUPS_PER_BLOCK = 8
HGRN_CHUNK = 48
HGRN_SUB = 16
HGRN_SAMPLE_SEQS = 8
MOE_ROWS = 512
MOE_TN = 256


def _cparams(n_axes):
    return pltpu.CompilerParams(dimension_semantics=("arbitrary",) * n_axes,
                                vmem_limit_bytes=VMEM_LIMIT)


def _pick(n, cands):
    for c in cands:
        if n % c == 0:
            return c
    raise ValueError(f"no tile for {n} in {cands}")


def _mm_kernel(*refs, has_res):
    if has_res:
        a_ref, w_ref, r_ref, o_ref, wc_ref = refs
    else:
        a_ref, w_ref, o_ref, wc_ref = refs
    m = pl.program_id(1)
    k = pl.program_id(2)

    @pl.when(m == 0)
    def _():
        wc_ref[k] = w_ref[0].astype(BF16)

    p = jnp.dot(a_ref[...], wc_ref[k], preferred_element_type=F32)

    @pl.when(k == 0)
    def _():
        o_ref[...] = (r_ref[...] + p) if has_res else p

    @pl.when(k > 0)
    def _():
        o_ref[...] += p


def mm(a, w, layer, res=None):
    M, K = a.shape
    N = w.shape[2]
    tm = _pick(M, (1280, 1024, 512, 256, 128))
    if K * N >= 4096 * 4096:
        tn = 1024
        tk = _pick(K, (2048, 1024)) if res is None else _pick(K, (1024,))
    else:
        tn = _pick(N, (1280, 1024, 512, 256, 128))
        tk = _pick(K, (1024, 512, 256, 128, K))
    nk = K // tk
    in_specs = [pl.BlockSpec((tm, tk), lambda n, m, k: (m, k)),
                pl.BlockSpec((1, tk, tn), lambda n, m, k: (layer, jnp.where(m == 0, k, nk - 1), n))]
    args = [a, w]
    if res is not None:
        in_specs.append(pl.BlockSpec((tm, tn), lambda n, m, k: (m, n)))
        args.append(res)
    return pl.pallas_call(
        functools.partial(_mm_kernel, has_res=res is not None),
        grid=(pl.cdiv(N, tn), M // tm, nk),
        in_specs=in_specs,
        out_specs=pl.BlockSpec((tm, tn), lambda n, m, k: (m, n)),
        out_shape=jax.ShapeDtypeStruct((M, N), F32),
        scratch_shapes=[pltpu.VMEM((nk, tk, tn), BF16)],
        compiler_params=_cparams(3),
    )(*args)


def _merge_kernel(ys_ref, yr_ref, yh_ref, ps_ref, pr_ref, ph_ref, g0_ref, g1_ref, g2_ref, o_ref,
                  cs_ref, cr_ref, ch_ref):
    @pl.when(pl.program_id(1) == 0)
    def _():
        cs_ref[...] = ps_ref[0].astype(BF16)
        cr_ref[...] = pr_ref[0].astype(BF16)
        ch_ref[...] = ph_ref[0].astype(BF16)

    acc = (jax.nn.sigmoid(g0_ref[...]) * jnp.dot(ys_ref[...], cs_ref[...], preferred_element_type=F32)
           + jax.nn.sigmoid(g1_ref[...]) * jnp.dot(yr_ref[...], cr_ref[...], preferred_element_type=F32)
           + jax.nn.sigmoid(g2_ref[...]) * jnp.dot(yh_ref[...], ch_ref[...], preferred_element_type=F32))
    o_ref[...] = acc.astype(BF16)


def merge_branches(z, gate_col0, ys, yr, yh, p_s5, p_rwkv, p_hgrn, layer):
    M = z.shape[0]
    D = p_s5.shape[2]
    tm = _pick(M, (1280, 1024, 512, 256, 128))
    tn = _pick(math.gcd(D, gate_col0), (256, 128))
    gb = gate_col0 // tn
    nd = D // tn

    def aspec(y):
        return pl.BlockSpec((tm, y.shape[1]), lambda n, m: (m, 0))

    def wspec(p):
        return pl.BlockSpec((1, p.shape[1], tn), lambda n, m: (layer, 0, n))

    def gspec(i):
        return pl.BlockSpec((tm, tn), lambda n, m: (m, gb + i * nd + n))

    return pl.pallas_call(
        _merge_kernel,
        grid=(nd, M // tm),
        in_specs=[aspec(ys), aspec(yr), aspec(yh), wspec(p_s5), wspec(p_rwkv), wspec(p_hgrn),
                  gspec(0), gspec(1), gspec(2)],
        out_specs=pl.BlockSpec((tm, tn), lambda n, m: (m, n)),
        out_shape=jax.ShapeDtypeStruct((M, D), BF16),
        scratch_shapes=[pltpu.VMEM((p.shape[1], tn), BF16) for p in (p_s5, p_rwkv, p_hgrn)],
        compiler_params=_cparams(2),
    )(ys, yr, yh, p_s5, p_rwkv, p_hgrn, z, z, z)


def _router_kernel(n_ref, w_ref, b_ref, o_ref, *, n_grp, n_exp):
    e_per = n_exp // n_grp
    logits = jnp.dot(n_ref[...], w_ref[...], precision=HI, preferred_element_type=F32) + b_ref[...]
    lane = lax.broadcasted_iota(jnp.int32, logits.shape, 1)
    lane_f = lane.astype(F32)
    big = float(4 * LANES)
    neg = -jnp.inf
    is_g = lane < n_grp
    lg = jnp.where(is_g, logits, neg)
    eg = jnp.exp(lg - jnp.max(lg, axis=-1, keepdims=True))
    pg = eg / jnp.sum(eg, axis=-1, keepdims=True)
    p_g = jnp.max(pg, axis=-1, keepdims=True)
    g_idx = jnp.min(jnp.where(is_g & (pg == p_g), lane_f, big), axis=-1, keepdims=True)
    e_f = lane_f - float(n_grp)
    in_sel = (e_f >= g_idx * e_per) & (e_f < (g_idx + 1.0) * e_per)
    le = jnp.where(in_sel, logits, neg)
    ee = jnp.exp(le - jnp.max(le, axis=-1, keepdims=True))
    pe = ee / jnp.sum(ee, axis=-1, keepdims=True)
    p1 = jnp.max(pe, axis=-1, keepdims=True)
    i1 = jnp.min(jnp.where(in_sel & (pe == p1), e_f, big), axis=-1, keepdims=True)
    rest = in_sel & (e_f != i1)
    p2 = jnp.max(jnp.where(rest, pe, -1.0), axis=-1, keepdims=True)
    i2 = jnp.min(jnp.where(rest & (pe == p2), e_f, big), axis=-1, keepdims=True)
    den = p1 + p2
    out = jnp.where(lane == 0, p_g * p1 / den,
                    jnp.where(lane == 1, p_g * p2 / den,
                              jnp.where(lane == 2, i1, jnp.where(lane == 3, i2, 0.0))))
    o_ref[...] = out


def router(n32, w, b, *, n_grp, n_exp):
    M, K = n32.shape
    tm = _pick(M, (256, 128))
    return pl.pallas_call(
        functools.partial(_router_kernel, n_grp=n_grp, n_exp=n_exp),
        grid=(M // tm,),
        in_specs=[pl.BlockSpec((tm, K), lambda m: (m, 0)),
                  pl.BlockSpec((K, LANES), lambda m: (0, 0)),
                  pl.BlockSpec((1, LANES), lambda m: (0, 0))],
        out_specs=pl.BlockSpec((tm, LANES), lambda m: (m, 0)),
        out_shape=jax.ShapeDtypeStruct((M, LANES), F32),
        compiler_params=_cparams(1),
    )(n32, w, b)


def _rms_kernel(x_ref, w_ref, *o_refs, want32, want16):
    x = x_ref[...]
    y = x * lax.rsqrt(jnp.mean(x * x, axis=-1, keepdims=True) + RMS_EPS) * w_ref[...]
    if want32:
        o_refs[0][...] = y
    if want16:
        o_refs[-1][...] = y.astype(BF16)


def rms_norm(x, w, *, want32, want16):
    M, D = x.shape
    tm = _pick(M, (256, 128))
    dts = ([F32] if want32 else []) + ([BF16] if want16 else [])
    return pl.pallas_call(
        functools.partial(_rms_kernel, want32=want32, want16=want16),
        grid=(M // tm,),
        in_specs=[pl.BlockSpec((tm, D), lambda m: (m, 0)),
                  pl.BlockSpec((1, D), lambda m: (0, 0))],
        out_specs=[pl.BlockSpec((tm, D), lambda m: (m, 0)) for _ in dts],
        out_shape=[jax.ShapeDtypeStruct((M, D), dt) for dt in dts],
        compiler_params=_cparams(1),
    )(x, w.reshape(1, D))


def _gelu_tanh(y):
    return 0.5 * y * (1.0 + jnp.tanh(math.sqrt(2.0 / math.pi) * (y + 0.044715 * (y * y * y))))


def _s5_kernel(u_ref, bm_ref, cm_ref, are_ref, aim_ref, d_ref, x0re_ref, x0im_ref,
               y_ref, xlre_ref, xlim_ref, xs_ref, sre_ref, sim_ref, *, steps, rows, half):
    tc = pl.program_id(2)

    @pl.when(tc == 0)
    def _():
        sre_ref[...] = x0re_ref[...]
        sim_ref[...] = x0im_ref[...]

    u = u_ref[...]
    bm = bm_ref[0]
    u_hi = u.astype(BF16)
    u_lo = (u - u_hi.astype(F32)).astype(BF16)
    b_hi = bm.astype(BF16)
    b_lo = (bm - b_hi.astype(F32)).astype(BF16)
    xs_ref[...] = (jnp.dot(u_hi, b_hi, preferred_element_type=F32)
                   + jnp.dot(u_hi, b_lo, preferred_element_type=F32)
                   + jnp.dot(u_lo, b_hi, preferred_element_type=F32))
    a_re = are_ref[0]
    a_im = aim_ref[0]

    def step(t, carry):
        x_re, x_im = carry
        r0 = pl.multiple_of(t * rows, rows)
        b_re = xs_ref[pl.ds(r0, rows), 0:half]
        b_im = xs_ref[pl.ds(r0, rows), half:2 * half]
        n_re = a_re * x_re - a_im * x_im + b_re
        n_im = a_re * x_im + a_im * x_re + b_im
        xs_ref[pl.ds(r0, rows), 0:half] = n_re
        xs_ref[pl.ds(r0, rows), half:2 * half] = n_im
        return n_re, n_im

    x_re, x_im = lax.fori_loop(0, steps, step, (sre_ref[...], sim_ref[...]), unroll=min(8, steps))
    sre_ref[...] = x_re
    sim_ref[...] = x_im
    xlre_ref[...] = x_re
    xlim_ref[...] = x_im
    y = jnp.dot(xs_ref[...].astype(BF16), cm_ref[0].astype(BF16), preferred_element_type=F32) + d_ref[0] * u
    y_ref[...] = _gelu_tanh(y)


def s5_scan(u2, bm, cm, a_re, a_im, dvec, x0_re, x0_im, *, groups, ntc, steps, rows):
    n_rows, W = u2.shape
    gbn = W // LANES
    half = bm.shape[2] // 2
    blk = steps * rows
    kern = functools.partial(_s5_kernel, steps=steps, rows=rows, half=half)
    return pl.pallas_call(
        kern,
        grid=(gbn, groups, ntc),
        in_specs=[pl.BlockSpec((blk, LANES), lambda g, s, t: (s * ntc + t, g)),
                  pl.BlockSpec((1, LANES, 2 * half), lambda g, s, t: (g, 0, 0)),
                  pl.BlockSpec((1, 2 * half, LANES), lambda g, s, t: (g, 0, 0)),
                  pl.BlockSpec((1, 1, half), lambda g, s, t: (g, 0, 0)),
                  pl.BlockSpec((1, 1, half), lambda g, s, t: (g, 0, 0)),
                  pl.BlockSpec((1, 1, LANES), lambda g, s, t: (g, 0, 0)),
                  pl.BlockSpec((rows, half), lambda g, s, t: (s, g)),
                  pl.BlockSpec((rows, half), lambda g, s, t: (s, g))],
        out_specs=[pl.BlockSpec((blk, LANES), lambda g, s, t: (s * ntc + t, g)),
                   pl.BlockSpec((rows, half), lambda g, s, t: (s, g)),
                   pl.BlockSpec((rows, half), lambda g, s, t: (s, g))],
        out_shape=[jax.ShapeDtypeStruct((n_rows, W), F32),
                   jax.ShapeDtypeStruct(x0_re.shape, F32),
                   jax.ShapeDtypeStruct(x0_im.shape, F32)],
        scratch_shapes=[pltpu.VMEM((blk, 2 * half), F32),
                        pltpu.VMEM((rows, half), F32),
                        pltpu.VMEM((rows, half), F32)],
        compiler_params=_cparams(3),
    )(u2, bm, cm, a_re, a_im, dvec, x0_re, x0_im)


def _s5_params(lam_re, lam_im, log_dt, b_re, b_im, c_re, c_im, d_skip):
    G, P, H = b_re.shape
    gpb = S5_GROUPS_PER_BLOCK
    gbn = G // gpb
    lam = lax.complex(lam_re.astype(F32), lam_im.astype(F32))
    dt = jnp.exp(log_dt.astype(F32))[:, None]
    a_bar = jnp.exp(lam * dt)
    b_bar = ((a_bar - 1.0) / lam)[..., None] * lax.complex(b_re.astype(F32), b_im.astype(F32))
    eye = jnp.eye(gpb, dtype=F32)

    def bdiag_in(m):
        m = m.reshape(gbn, gpb, P, H).transpose(0, 1, 3, 2)
        return jnp.einsum('aghp,gk->aghkp', m, eye).reshape(gbn, gpb * H, gpb * P)

    def bdiag_out(m):
        m = m.reshape(gbn, gpb, H, P)
        return jnp.einsum('aghp,gk->agpkh', m, eye).reshape(gbn, gpb * P, gpb * H)

    bm = jnp.concatenate([bdiag_in(jnp.real(b_bar)), bdiag_in(jnp.imag(b_bar))], axis=2)
    cm = jnp.concatenate([bdiag_out(c_re.astype(F32)), bdiag_out(-c_im.astype(F32))], axis=1)
    a_re = jnp.real(a_bar).reshape(gbn, 1, gpb * P)
    a_im = jnp.imag(a_bar).reshape(gbn, 1, gpb * P)
    dvec = d_skip.astype(F32).reshape(gbn, 1, gpb * H)
    return bm, cm, a_re, a_im, dvec


def _pair_to_heads(x0, x1):
    n = LANES // 2
    at = jnp.concatenate([x0, x1], axis=0).T
    top, bot = at[0:n], at[n:2 * n]
    lo = lax.broadcasted_iota(jnp.int32, top.shape, 1) < n
    return (jnp.where(lo, top, pltpu.roll(bot, n, axis=1)),
            jnp.where(lo, pltpu.roll(top, n, axis=1), bot))


def _pair_from_heads(y0, y1):
    n = LANES // 2
    lo = lax.broadcasted_iota(jnp.int32, y0.shape, 1) < n
    top = jnp.where(lo, y0, pltpu.roll(y1, n, axis=1))
    bot = jnp.where(lo, pltpu.roll(y0, n, axis=1), y1)
    a = jnp.concatenate([top, bot], axis=0).T
    return a[0:n], a[n:2 * n]


def _wkv_kernel(kk_ref, w_ref, b_ref, k_ref, r_ref, v_ref, s0_ref, y_ref, so_ref,
                s_ref, sa_ref, tl_ref, yt_ref, *, steps, n):
    @pl.when(pl.program_id(1) == 0)
    def _():
        s_ref[...] = s0_ref[...]

    ins = (kk_ref, w_ref, b_ref, k_ref, r_ref, v_ref)

    def relayout_in(pi, _):
        t0 = 2 * pi
        for a, ref in enumerate(ins):
            h0, h1 = _pair_to_heads(ref[t0], ref[t0 + 1])
            tl_ref[a, t0] = h0
            tl_ref[a, t0 + 1] = h1
        return 0

    lax.fori_loop(0, steps // 2, relayout_in, 0, unroll=2)

    def tstep(t, _):
        kk = tl_ref[0, t]

        def sa_pass(v, _):
            sa_ref[pl.ds(v, 1), :] = -jnp.sum(s_ref[v] * kk, axis=0, keepdims=True)
            return 0

        lax.fori_loop(0, n, sa_pass, 0, unroll=8)
        w = tl_ref[1, t]
        b = tl_ref[2, t]
        kx = tl_ref[3, t]
        r = tl_ref[4, t]

        def update_pass(v, _):
            sn = s_ref[v] * w + sa_ref[pl.ds(v, 1), :] * b + tl_ref[5, t, pl.ds(v, 1), :] * kx
            s_ref[v] = sn
            yt_ref[t, pl.ds(v, 1), :] = jnp.sum(sn * r, axis=0, keepdims=True)
            return 0

        lax.fori_loop(0, n, update_pass, 0, unroll=8)
        return 0

    lax.fori_loop(0, steps, tstep, 0)

    def relayout_out(pi, _):
        t0 = 2 * pi
        y0, y1 = _pair_from_heads(yt_ref[t0], yt_ref[t0 + 1])
        y_ref[t0] = y0
        y_ref[t0 + 1] = y1
        return 0

    lax.fori_loop(0, steps // 2, relayout_out, 0, unroll=2)
    so_ref[...] = s_ref[...]


def wkv7(seqs, s0):
    NT, R, _ = seqs[0].shape
    n = LANES // 2
    groups = R // n
    steps = _pick(NT, (48, 16, 8, 4, 2))
    seq_spec = pl.BlockSpec((steps, n, LANES), lambda g, t: (t, g, 0))
    st_spec = pl.BlockSpec((n, n, LANES), lambda g, t: (0, 0, g))
    return pl.pallas_call(
        functools.partial(_wkv_kernel, steps=steps, n=n),
        grid=(groups, NT // steps),
        in_specs=[seq_spec] * 6 + [st_spec],
        out_specs=[seq_spec, st_spec],
        out_shape=[jax.ShapeDtypeStruct((NT, R, LANES), F32), jax.ShapeDtypeStruct((n, n, groups * LANES), F32)],
        scratch_shapes=[pltpu.VMEM((n, n, LANES), F32), pltpu.VMEM((n, LANES), F32),
                        pltpu.VMEM((6, steps, n, LANES), F32), pltpu.VMEM((steps, n, LANES), F32)],
        compiler_params=_cparams(2),
    )(*seqs, s0)


def _head_sum(x, e_ref):
    hi = x.astype(BF16)
    lo = (x - hi.astype(F32)).astype(BF16)
    e = e_ref[...]
    return jnp.dot(hi, e, preferred_element_type=F32) + jnp.dot(lo, e, preferred_element_type=F32)


def _rwkv_prep_kernel(r_ref, k_ref, v_ref, xw_ref, xa_ref, xg0_ref, xg1_ref, wup_ref, aup_ref, gup_ref,
                      w0_ref, a0_ref, kk_w_ref, ka_ref, e_ref,
                      kk_o, w_o, b_o, k_o, r_o, v_o, g_o):
    r_o[...] = r_ref[...]
    v_o[...] = v_ref[...]
    k = k_ref[...]
    lw = w0_ref[...] + jnp.dot(jnp.tanh(xw_ref[...]).astype(BF16), wup_ref[0].astype(BF16),
                               preferred_element_type=F32)
    w_log = -(jnp.maximum(-lw, 0.0) + jnp.log1p(jnp.exp(-jnp.abs(lw)))) - 0.5
    w_o[...] = jnp.exp(-jnp.exp(w_log))
    a = jax.nn.sigmoid(a0_ref[...] + jnp.dot(xa_ref[...].astype(BF16), aup_ref[0].astype(BF16),
                                             preferred_element_type=F32))
    xg = jnp.concatenate([xg0_ref[...], xg1_ref[...]], axis=1)
    g_o[...] = jnp.dot(jax.nn.sigmoid(xg).astype(BF16), gup_ref[0].astype(BF16), preferred_element_type=F32)
    kx = k * kk_w_ref[...]
    kk = kx / jnp.maximum(jnp.sqrt(_head_sum(kx * kx, e_ref)), 1e-12)
    kk_o[...] = kk
    b_o[...] = kk * a
    k_o[...] = k * (1.0 + (a - 1.0) * ka_ref[...])


def _seq_specs(tm, half, nh, nt, time_major):
    if time_major:
        return pl.BlockSpec((tm, half), lambda b, i, hf: (i, b * nh + hf)), lambda nb: (nt * tm, nb * nh * half)
    return pl.BlockSpec((tm, half), lambda b, i, hf: (b * nt + i, hf)), lambda nb: (nb * nt * tm, nh * half)


def rwkv_prep(zm, p, layer, e_mat, *, wr, row_blk0, nb, nt, tm, time_major):
    half = e_mat.shape[0]
    nh = wr // half
    rk_d = p['w_up'].shape[1]
    ra_d = p['a_up'].shape[1]
    rg_d = p['g_up'].shape[1]
    assert rk_d == LANES and ra_d == LANES and rg_d % (2 * LANES) == 0 and wr % half == 0
    xw_blk = 3 * wr // LANES
    xg_blk = (3 * wr + rk_d + ra_d) // (rg_d // 2)
    assert (3 * wr + rk_d + ra_d) % (rg_d // 2) == 0

    def zspec(width, col):
        return pl.BlockSpec((tm, width), lambda b, i, hf: (row_blk0 + b * nt + i, col(hf)))

    def vec(x):
        return x.astype(F32).reshape(1, wr)

    vspec = pl.BlockSpec((1, half), lambda b, i, hf: (0, hf))
    seq_spec, seq_shape = _seq_specs(tm, half, nh, nt, time_major)
    g_spec, g_shape = _seq_specs(tm, half, nh, nt, False)
    return pl.pallas_call(
        _rwkv_prep_kernel,
        grid=(nb, nt, nh),
        in_specs=[zspec(half, lambda hf: hf), zspec(half, lambda hf: nh + hf), zspec(half, lambda hf: 2 * nh + hf),
                  zspec(LANES, lambda hf: xw_blk), zspec(LANES, lambda hf: xw_blk + 1),
                  zspec(rg_d // 2, lambda hf: xg_blk), zspec(rg_d // 2, lambda hf: xg_blk + 1),
                  pl.BlockSpec((1, rk_d, half), lambda b, i, hf: (layer, 0, hf)),
                  pl.BlockSpec((1, ra_d, half), lambda b, i, hf: (layer, 0, hf)),
                  pl.BlockSpec((1, rg_d, half), lambda b, i, hf: (layer, 0, hf)),
                  vspec, vspec, vspec, vspec,
                  pl.BlockSpec((half, half), lambda b, i, hf: (0, 0))],
        out_specs=[seq_spec] * 6 + [g_spec],
        out_shape=[jax.ShapeDtypeStruct(seq_shape(nb), F32)] * 6 + [jax.ShapeDtypeStruct(g_shape(nb), F32)],
        compiler_params=_cparams(3),
    )(zm, zm, zm, zm, zm, zm, zm, p['w_up'], p['a_up'], p['g_up'],
      vec(p['w0']), vec(p['a0']), vec(p['k_k']), vec(p['k_a']), e_mat)


def _rwkv_post_kernel(y_ref, r_ref, k_ref, v_ref, g_ref, lnw_ref, lnb_ref, rk_ref, e_ref, o_ref, *, n):
    y = y_ref[...]
    d = y - _head_sum(y, e_ref) * (1.0 / n)
    var = _head_sum(d * d, e_ref) * (1.0 / n)
    yn = d * lax.rsqrt(var + RWKV_LN_EPS) * lnw_ref[...] + lnb_ref[...]
    bonus = _head_sum(r_ref[...] * k_ref[...] * rk_ref[...], e_ref) * v_ref[...]
    o_ref[...] = ((yn + bonus) * g_ref[...]).astype(BF16)


def rwkv_post(y, zm, k2, g, p, e_mat, *, wr, n, row_blk0, nb, nt, tm, time_major):
    half = e_mat.shape[0]
    nh = wr // half
    seq_spec, _ = _seq_specs(tm, half, nh, nt, time_major)
    bm_spec, bm_shape = _seq_specs(tm, half, nh, nt, False)
    vspec = pl.BlockSpec((1, half), lambda b, i, hf: (0, hf))

    def zspec(sec):
        return pl.BlockSpec((tm, half), lambda b, i, hf: (row_blk0 + b * nt + i, sec * nh + hf))

    def vec(x):
        return x.astype(F32).reshape(1, wr)

    return pl.pallas_call(
        functools.partial(_rwkv_post_kernel, n=n),
        grid=(nb, nt, nh),
        in_specs=[seq_spec, zspec(0), seq_spec, zspec(2), bm_spec, vspec, vspec, vspec,
                  pl.BlockSpec((half, half), lambda b, i, hf: (0, 0))],
        out_specs=bm_spec,
        out_shape=jax.ShapeDtypeStruct(bm_shape(nb), BF16),
        compiler_params=_cparams(3),
    )(y, zm, k2, zm, g, vec(p['ln_w']), vec(p['ln_b']), vec(p['r_k']), e_mat)


def _hgrn_gates(zq, zf, loglb, log1mlb, onemlb):
    q = zq * jax.nn.sigmoid(zq)
    log_sig = jnp.minimum(zf, 0.0) - jnp.log1p(jnp.exp(-jnp.abs(zf)))
    t = log1mlb + log_sig
    hi = jnp.maximum(loglb, t)
    log_f = hi + jnp.log1p(jnp.exp(-jnp.abs(loglb - t)))
    k = onemlb * jax.nn.sigmoid(-zf)
    return q, log_f, k


def _hgrn_out(o, zg, nw):
    o = o * lax.rsqrt(jnp.mean(o * o, axis=-1, keepdims=True) + RMS_EPS)
    return o * nw * (zg * jax.nn.sigmoid(zg))


def _col_bcast(row, n_lanes):
    e = jnp.concatenate([row, jnp.zeros((SUBLANES - 1, row.shape[1]), F32)], axis=0)
    ones = jnp.concatenate([jnp.ones((1, n_lanes), F32), jnp.zeros((SUBLANES - 1, n_lanes), F32)], axis=0)
    return lax.dot_general(e, ones, (((0,), (0,)), ((), ())), precision=HI, preferred_element_type=F32)


def _hgrn_prompt_kernel(*refs, rows, sub, heads):
    z_refs = refs[:4 * heads]
    loglb_ref, log1mlb_ref, onemlb_ref, nw_ref, o_ref, so_ref, s_ref = refs[4 * heads:]

    @pl.when(pl.program_id(1) == 0)
    def _():
        s_ref[...] = jnp.zeros_like(s_ref)

    dk = s_ref.shape[1]
    for h in range(heads):
        zq_ref, zf_ref, zi_ref, zg_ref = (z_refs[sec * heads + h] for sec in range(4))
        o, s_new = _hgrn_chunk(zq_ref[...], zf_ref[...], zi_ref[...], loglb_ref[h], log1mlb_ref[h],
                               onemlb_ref[h], s_ref[h], rows=rows, sub=sub)
        s_ref[h] = s_new
        so_ref[0, h] = s_new
        o_ref[:, h * dk:(h + 1) * dk] = _hgrn_out(o, zg_ref[...], nw_ref[h])


_NN = (((1,), (0,)), ((), ()))
_NT = (((1,), (1,)), ((), ()))
_TN = (((0,), (0,)), ((), ()))


def _dg3(a, b, dims):
    a_hi = a.astype(BF16)
    a_lo = (a - a_hi.astype(F32)).astype(BF16)
    b_hi = b.astype(BF16)
    b_lo = (b - b_hi.astype(F32)).astype(BF16)

    def f(x, y):
        return lax.dot_general(x, y, dims, preferred_element_type=F32)

    return f(a_hi, b_hi) + f(a_hi, b_lo) + f(a_lo, b_hi)


def _hgrn_chunk(zq, zf, v, loglb, log1mlb, onemlb, s_old, *, rows, sub):
    q, log_f, k = _hgrn_gates(zq, zf, loglb, log1mlb, onemlb)
    dv = v.shape[1]
    ri = lax.broadcasted_iota(jnp.int32, (rows, rows), 0)
    ci = lax.broadcasted_iota(jnp.int32, (rows, rows), 1)
    tri = (ci <= ri).astype(F32)
    bcum = jnp.dot(tri, log_f, precision=HI, preferred_element_type=F32)
    o_state = _dg3(q * jnp.exp(bcum), s_old, _NN)
    o_parts = []
    for i in range(rows // sub):
        lo, hi = i * sub, (i + 1) * sub
        b_i = bcum[lo:hi]
        q_i = q[lo:hi]
        o_i = o_state[lo:hi]
        if i > 0:
            b_ref_row = bcum[lo - 1:lo]
            qe = q_i * jnp.exp(b_i - b_ref_row)
            ke = k[0:lo] * jnp.exp(b_ref_row - bcum[0:lo])
            o_i = o_i + _dg3(_dg3(qe, ke, _NT), v[0:lo], _NN)
        t_idx = lax.broadcasted_iota(jnp.int32, (sub, 1), 0)
        for s in range(sub):
            e_s = jnp.exp(jnp.minimum(b_i - b_i[s:s + 1], 0.0))
            col = jnp.sum(q_i * k[lo + s:lo + s + 1] * e_s, axis=-1, keepdims=True)
            col = jnp.where(t_idx >= s, col, 0.0)
            o_i = o_i + col * v[lo + s:lo + s + 1]
        o_parts.append(o_i)
    o = jnp.concatenate(o_parts, axis=0)
    b_end = bcum[rows - 1:rows]
    ke_end = k * jnp.exp(b_end - bcum)
    s_new = _col_bcast(jnp.exp(b_end), dv) * s_old + _dg3(ke_end, v, _TN)
    return o, s_new


def _hgrn_sample_kernel(zq_ref, zf_ref, zi_ref, zg_ref, loglb_ref, log1mlb_ref, onemlb_ref, nw_ref, s0_ref,
                        o_ref, so_ref, *, nseq, slen):
    rows = nseq * slen
    q, log_f, k = _hgrn_gates(zq_ref[...], zf_ref[...], loglb_ref[0], log1mlb_ref[0], onemlb_ref[0])
    v = zi_ref[...]
    dv = v.shape[1]
    ri = lax.broadcasted_iota(jnp.int32, (rows, rows), 0)
    ci = lax.broadcasted_iota(jnp.int32, (rows, rows), 1)
    same = (ri // slen) == (ci // slen)
    tri = ((ci <= ri) & same).astype(F32)
    bcum = jnp.dot(tri, log_f, precision=HI, preferred_element_type=F32)
    t_idx = lax.broadcasted_iota(jnp.int32, (rows, 1), 0)
    o = jnp.zeros((rows, dv), F32)
    for s in range(rows):
        e_s = jnp.exp(jnp.minimum(bcum - bcum[s:s + 1], 0.0))
        col = jnp.sum(q * k[s:s + 1] * e_s, axis=-1, keepdims=True)
        col = jnp.where((t_idx >= s) & (t_idx < (s // slen + 1) * slen), col, 0.0)
        o = o + col * v[s:s + 1]
    qe = q * jnp.exp(bcum)
    for j in range(nseq):
        in_seq = (t_idx >= j * slen) & (t_idx < (j + 1) * slen)
        s_old = s0_ref[j, 0]
        o = o + _dg3(jnp.where(in_seq, qe, 0.0), s_old, _NN)
        b_end = bcum[(j + 1) * slen - 1:(j + 1) * slen]
        ke = jnp.where(in_seq, k * jnp.exp(b_end - bcum), 0.0)
        so_ref[j, 0] = _col_bcast(jnp.exp(b_end), dv) * s_old + _dg3(ke, v, _TN)
    o_ref[...] = _hgrn_out(o, zg_ref[...], nw_ref[0])


def hgrn2(z, col0, lb, norm_w, s0_sample, *, bp, lp, bs, ls, heads, dk):
    assert dk == LANES and col0 % LANES == 0
    cb = col0 // LANES
    loglb = jnp.log(lb).reshape(heads, 1, dk)
    log1mlb = jnp.log1p(-lb).reshape(heads, 1, dk)
    onemlb = (1.0 - lb).reshape(heads, 1, dk)
    nw = norm_w.astype(F32).reshape(heads, 1, dk)
    W = heads * dk
    par_full = pl.BlockSpec((heads, 1, dk), lambda b, c: (0, 0, 0))

    rc = HGRN_CHUNK
    assert lp % rc == 0
    nch = lp // rc

    def zspec(col):
        return pl.BlockSpec((rc, dk), lambda b, c: (b * nch + c, cb + col))

    o_p, s_p = pl.pallas_call(
        functools.partial(_hgrn_prompt_kernel, rows=rc, sub=HGRN_SUB, heads=heads),
        grid=(bp, nch),
        in_specs=[zspec(col) for col in range(4 * heads)] + [par_full] * 4,
        out_specs=[pl.BlockSpec((rc, W), lambda b, c: (b * nch + c, 0)),
                   pl.BlockSpec((1, heads, dk, dk), lambda b, c: (b, 0, 0, 0))],
        out_shape=[jax.ShapeDtypeStruct((bp * lp, W), F32),
                   jax.ShapeDtypeStruct((bp, heads, dk, dk), F32)],
        scratch_shapes=[pltpu.VMEM((heads, dk, dk), F32)],
        compiler_params=_cparams(2),
    )(*([z] * (4 * heads)), loglb, log1mlb, onemlb, nw)

    nseq = HGRN_SAMPLE_SEQS
    rs = nseq * ls
    assert bs % nseq == 0 and rs % SUBLANES == 0 and (bp * lp) % rs == 0
    r0 = bp * lp // rs
    par_spec2 = pl.BlockSpec((1, 1, dk), lambda p, h: (h, 0, 0))

    def zspec_s(sec):
        return pl.BlockSpec((rs, dk), lambda p, h: (r0 + p, cb + sec * heads + h))

    st_spec = pl.BlockSpec((nseq, 1, dk, dk), lambda p, h: (p, h, 0, 0))
    o_s, s_s = pl.pallas_call(
        functools.partial(_hgrn_sample_kernel, nseq=nseq, slen=ls),
        grid=(bs // nseq, heads),
        in_specs=[zspec_s(0), zspec_s(1), zspec_s(2), zspec_s(3), par_spec2, par_spec2, par_spec2, par_spec2,
                  st_spec],
        out_specs=[pl.BlockSpec((rs, dk), lambda p, h: (p, h)), st_spec],
        out_shape=[jax.ShapeDtypeStruct((bs * ls, W), F32),
                   jax.ShapeDtypeStruct((bs, heads, dk, dk), F32)],
        compiler_params=_cparams(2),
    )(z, z, z, z, loglb, log1mlb, onemlb, nw, s0_sample.astype(F32))
    return o_p, o_s, s_p, s_s


def _moe_up_kernel(be_ref, nb_ref, x_ref, wg_ref, wu_ref, h_ref, wgc_ref, wuc_ref):
    i = pl.program_id(1)
    valid = i < nb_ref[0]
    prev = be_ref[jnp.maximum(i - 1, 0)]
    fresh = valid & ((i == 0) | (be_ref[i] != prev))

    @pl.when(fresh)
    def _():
        wgc_ref[...] = wg_ref[0, 0].astype(BF16)
        wuc_ref[...] = wu_ref[0, 0].astype(BF16)

    @pl.when(valid)
    def _():
        x = x_ref[...]
        g = jnp.dot(x, wgc_ref[...], preferred_element_type=F32)
        u = jnp.dot(x, wuc_ref[...], preferred_element_type=F32)
        h_ref[...] = (g * jax.nn.sigmoid(g) * u).astype(BF16)

    @pl.when(jnp.logical_not(valid))
    def _():
        h_ref[...] = jnp.zeros_like(h_ref)


def _moe_down_kernel(be_ref, nb_ref, h_ref, wd_ref, y_ref, wdc_ref):
    i = pl.program_id(1)
    valid = i < nb_ref[0]
    prev = be_ref[jnp.maximum(i - 1, 0)]
    fresh = valid & ((i == 0) | (be_ref[i] != prev))

    @pl.when(fresh)
    def _():
        wdc_ref[...] = wd_ref[0, 0].astype(BF16)

    @pl.when(valid)
    def _():
        y_ref[...] = jnp.dot(h_ref[...], wdc_ref[...], preferred_element_type=F32)

    @pl.when(jnp.logical_not(valid))
    def _():
        y_ref[...] = jnp.zeros_like(y_ref)


def _row_copy(src_ref, row, dst_ref, slot, sem):
    return pltpu.make_async_copy(src_ref.at[pl.ds(row, 1)], dst_ref.at[pl.ds(slot, 1)], sem)


def _moe_gather_kernel(nb_ref, tok_ref, src_ref, o_ref, buf_ref, sem):
    i = pl.program_id(0)
    rows = buf_ref.shape[0]

    @pl.when(i < nb_ref[0])
    def _():
        def issue(r, _):
            _row_copy(src_ref, tok_ref[0, 0, r], buf_ref, r, sem).start()
            return 0

        lax.fori_loop(0, rows, issue, 0)

        def drain(r, _):
            _row_copy(src_ref, 0, buf_ref, r, sem).wait()
            return 0

        lax.fori_loop(0, rows, drain, 0)
        o_ref[...] = buf_ref[...].astype(BF16)

    @pl.when(i >= nb_ref[0])
    def _():
        o_ref[...] = jnp.zeros_like(o_ref)


def moe_gather(n32, row_tok, n_used):
    R = row_tok.shape[0]
    D = n32.shape[1]
    nblk = R // MOE_ROWS
    return pl.pallas_call(
        _moe_gather_kernel,
        grid_spec=pltpu.PrefetchScalarGridSpec(
            num_scalar_prefetch=1,
            grid=(nblk,),
            in_specs=[pl.BlockSpec((1, 1, MOE_ROWS), lambda i, nb: (i, 0, 0), memory_space=pltpu.SMEM),
                      pl.BlockSpec(memory_space=pl.ANY)],
            out_specs=pl.BlockSpec((MOE_ROWS, D), lambda i, nb: (i, 0)),
            scratch_shapes=[pltpu.VMEM((MOE_ROWS, D), F32), pltpu.SemaphoreType.DMA(())]),
        out_shape=jax.ShapeDtypeStruct((R, D), BF16),
        compiler_params=_cparams(1),
    )(n_used, row_tok.reshape(nblk, 1, MOE_ROWS), n32)


def _moe_combine_kernel(slot_ref, yb_ref, h_ref, g_ref, o_ref, buf_ref, sem, *, t_real):
    i = pl.program_id(0)
    rows = h_ref.shape[0]

    def issue(r, _):
        _row_copy(yb_ref, slot_ref[0, 0, r], buf_ref, r, sem).start()
        return 0

    lax.fori_loop(0, TOP_K * rows, issue, 0)

    def drain(r, _):
        _row_copy(yb_ref, 0, buf_ref, r, sem).wait()
        return 0

    lax.fori_loop(0, TOP_K * rows, drain, 0)
    g = g_ref[...]
    tok = i * rows + lax.broadcasted_iota(jnp.int32, (rows, 1), 0)
    live = tok < t_real
    y = h_ref[...]
    for k in range(TOP_K):
        y = y + jnp.where(live, g[:, k:k + 1], 0.0) * buf_ref[k * rows:(k + 1) * rows]
    o_ref[...] = y


def moe_combine(yb, slot, h, routed, t_real):
    Tp, D = h.shape
    rows = _pick(Tp, (256, 128))
    nblk = Tp // rows
    slot_blk = slot.reshape(nblk, rows, TOP_K).transpose(0, 2, 1).reshape(nblk, 1, TOP_K * rows)
    return pl.pallas_call(
        functools.partial(_moe_combine_kernel, t_real=t_real),
        grid=(nblk,),
        in_specs=[pl.BlockSpec((1, 1, TOP_K * rows), lambda i: (i, 0, 0), memory_space=pltpu.SMEM),
                  pl.BlockSpec(memory_space=pl.ANY),
                  pl.BlockSpec((rows, D), lambda i: (i, 0)),
                  pl.BlockSpec((rows, LANES), lambda i: (i, 0))],
        out_specs=pl.BlockSpec((rows, D), lambda i: (i, 0)),
        out_shape=jax.ShapeDtypeStruct((Tp, D), F32),
        scratch_shapes=[pltpu.VMEM((TOP_K * rows, D), F32), pltpu.SemaphoreType.DMA(())],
        compiler_params=_cparams(1),
    )(slot_blk, yb, h, routed)


def moe_experts(xs, blk_expert, n_used, w_gate, w_up, w_down, layer):
    R, D = xs.shape
    DE = w_gate.shape[3]
    nblk = R // MOE_ROWS
    tn = _pick(DE, (MOE_TN, 128))
    hid = pl.pallas_call(
        _moe_up_kernel,
        grid_spec=pltpu.PrefetchScalarGridSpec(
            num_scalar_prefetch=2,
            grid=(DE // tn, nblk),
            in_specs=[pl.BlockSpec((MOE_ROWS, D), lambda j, i, be, nb: (jnp.minimum(i, nb[0] - 1), 0)),
                      pl.BlockSpec((1, 1, D, tn), lambda j, i, be, nb: (layer, be[i], 0, j)),
                      pl.BlockSpec((1, 1, D, tn), lambda j, i, be, nb: (layer, be[i], 0, j))],
            out_specs=pl.BlockSpec((MOE_ROWS, tn), lambda j, i, be, nb: (i, j)),
            scratch_shapes=[pltpu.VMEM((D, tn), BF16), pltpu.VMEM((D, tn), BF16)]),
        out_shape=jax.ShapeDtypeStruct((R, DE), BF16),
        compiler_params=_cparams(2),
    )(blk_expert, n_used, xs, w_gate, w_up)
    tn2 = _pick(D, (1024, 512, 256, 128))
    return pl.pallas_call(
        _moe_down_kernel,
        grid_spec=pltpu.PrefetchScalarGridSpec(
            num_scalar_prefetch=2,
            grid=(D // tn2, nblk),
            in_specs=[pl.BlockSpec((MOE_ROWS, DE), lambda j, i, be, nb: (jnp.minimum(i, nb[0] - 1), 0)),
                      pl.BlockSpec((1, 1, DE, tn2), lambda j, i, be, nb: (layer, be[i], 0, j))],
            out_specs=pl.BlockSpec((MOE_ROWS, tn2), lambda j, i, be, nb: (i, j)),
            scratch_shapes=[pltpu.VMEM((DE, tn2), BF16)]),
        out_shape=jax.ShapeDtypeStruct((R, D), F32),
        compiler_params=_cparams(2),
    )(blk_expert, n_used, hid, w_down)


def moe_ffn(h, n32, t_real, rg_w, rg_b, re_w, re_b, w_gate, w_up, w_down, layer):
    Tp, D = n32.shape
    n_grp = rg_w.shape[1]
    n_exp = re_w.shape[1]
    wr = jnp.concatenate([rg_w, re_w], axis=1).astype(F32)
    wr = jnp.pad(wr, ((0, 0), (0, LANES - wr.shape[1])))
    br = jnp.concatenate([rg_b, re_b]).astype(F32)
    br = jnp.pad(br, (0, LANES - br.shape[0])).reshape(1, LANES)
    routed = router(n32, wr, br, n_grp=n_grp, n_exp=n_exp)
    expert = routed[:t_real, TOP_K:2 * TOP_K].astype(jnp.int32)

    tk = t_real * TOP_K
    flat_e = expert.reshape(tk).astype(jnp.int32)
    order = jnp.argsort(flat_e)
    e_sorted = flat_e[order]
    counts = jnp.bincount(flat_e, length=n_exp)
    padded = (counts + MOE_ROWS - 1) // MOE_ROWS * MOE_ROWS
    pad_end = jnp.cumsum(padded)
    pad_start = pad_end - padded
    start = jnp.cumsum(counts) - counts
    dest = (pad_start[e_sorted] + jnp.arange(tk, dtype=jnp.int32) - start[e_sorted]).astype(jnp.int32)
    n_blocks = -(-tk // MOE_ROWS) + n_exp
    R = n_blocks * MOE_ROWS
    row_tok = jnp.zeros((R,), jnp.int32).at[dest].set((order // TOP_K).astype(jnp.int32))
    n_used = (pad_end[-1] // MOE_ROWS).astype(jnp.int32)
    blk_start = jnp.arange(n_blocks, dtype=jnp.int32) * MOE_ROWS
    blk_start = jnp.minimum(blk_start, (n_used - 1) * MOE_ROWS)
    blk_expert = jnp.minimum(jnp.searchsorted(pad_end, blk_start, side='right'), n_exp - 1).astype(jnp.int32)
    xs = moe_gather(n32, row_tok, n_used.reshape(1))
    yb = moe_experts(xs, blk_expert, n_used.reshape(1), w_gate, w_up, w_down, layer)
    slot = jnp.zeros((Tp * TOP_K,), jnp.int32).at[order].set(dest).reshape(Tp, TOP_K)
    return moe_combine(yb, slot, h, routed, t_real)


def _split_groups(x, bp, lp, bs, ls):
    n = x.shape[-1]
    xp = x[:bp * lp].reshape(bp, lp, n)
    xs = x[bp * lp:bp * lp + bs * ls].reshape(bs, ls, n)
    return xp, xs


def _s5_branch(u, prm, x0re_s, x0im_s, w_glu, b_glu, layer, *, bp, lp, bs, ls, tp):
    W = u.shape[1]
    bm, cm, a_re, a_im, dvec = prm
    n_state = x0re_s.shape[1] * x0re_s.shape[2]
    up, us = _split_groups(u, bp, lp, bs, ls)
    rows_p = SUBLANES
    steps_p = _pick(lp, (344, 48, 16, 8))
    up_tm = jnp.pad(up, ((0, rows_p - bp), (0, 0), (0, 0))).transpose(1, 0, 2).reshape(lp * rows_p, W)
    zero = jnp.zeros((rows_p, n_state), F32)
    yp, pre, pim = s5_scan(up_tm, bm, cm, a_re, a_im, dvec, zero, zero,
                           groups=1, ntc=lp // steps_p, steps=steps_p, rows=rows_p)
    yp = yp.reshape(lp, rows_p, W).transpose(1, 0, 2)[:bp].reshape(bp * lp, W)
    rows_s = _pick(bs, (32, 16, 8))
    ng = bs // rows_s
    us_tm = us.reshape(ng, rows_s, ls, W).transpose(0, 2, 1, 3).reshape(bs * ls, W)
    ys, sre, sim = s5_scan(us_tm, bm, cm, a_re, a_im, dvec,
                           x0re_s.reshape(bs, n_state).astype(F32), x0im_s.reshape(bs, n_state).astype(F32),
                           groups=ng, ntc=1, steps=ls, rows=rows_s)
    ys = ys.reshape(ng, ls, rows_s, W).transpose(0, 2, 1, 3).reshape(bs * ls, W)
    y = jnp.concatenate([yp, ys, jnp.zeros((tp - bp * lp - bs * ls, W), F32)], axis=0)
    gl = mm(y.astype(BF16), w_glu, layer)
    y = y * jax.nn.sigmoid(gl + b_glu.astype(F32))
    gshape = x0re_s.shape[1:]
    return (y, pre[:bp].reshape((bp,) + gshape), pim[:bp].reshape((bp,) + gshape),
            sre.reshape((bs,) + gshape), sim.reshape((bs,) + gshape))


def _wkv_state_in(s, heads, n):
    nb = s.shape[0]
    per = LANES // heads
    s = s.astype(F32).reshape(nb // per, per, heads // 2, 2, n, n)
    return s.transpose(4, 5, 0, 3, 1, 2).reshape(n, n, nb * heads)


def _wkv_state_out(s, nb, heads, n):
    per = LANES // heads
    s = s.reshape(n, n, nb // per, 2, per, heads // 2)
    return s.transpose(2, 4, 5, 3, 0, 1).reshape(nb, heads, n, n)


def _rwkv_branch(zr, prev_s, wkv0_s, p, layer, *, bp, lp, bs, ls, tp, heads, n):
    wr = heads * n
    assert n == LANES // 2 and LANES % heads == 0 and bp % (LANES // heads) == 0 and bs % (LANES // heads) == 0
    zp, zs = _split_groups(zr, bp, lp, bs, ls)
    sh_p = jnp.concatenate([jnp.zeros((bp, 1, zr.shape[1]), F32), zp[:, :-1]], axis=1)
    sh_s = jnp.concatenate([prev_s[:, None].astype(F32), zs[:, :-1]], axis=1)
    t_real = bp * lp + bs * ls
    shifted = jnp.concatenate([sh_p.reshape(bp * lp, -1), sh_s.reshape(bs * ls, -1),
                               jnp.zeros((tp - t_real, zr.shape[1]), F32)], axis=0)
    zm = zr + (shifted - zr) * p['mu'].astype(F32)
    half = math.gcd(wr, 1024)
    hid = jnp.arange(half, dtype=jnp.int32) // n
    e_mat = (hid[:, None] == hid[None, :]).astype(BF16)

    tm_p = _pick(lp, (344, 96, 48, 16, 8))
    geo_p = dict(wr=wr, row_blk0=0, nb=bp, nt=lp // tm_p, tm=tm_p, time_major=True)
    *seq_p, g_p = rwkv_prep(zm, p, layer, e_mat, **geo_p)
    y_p, s_p = wkv7([t.reshape(lp, bp * wr // LANES, LANES) for t in seq_p], jnp.zeros((n, n, bp * heads), F32))
    y_p = rwkv_post(y_p.reshape(lp, bp * wr), zm, seq_p[3], g_p, p, e_mat, n=n, **geo_p)

    tm_s = _pick(math.gcd(bp * lp, bs * ls), (256, 128, 64, 32, 16, 8))
    geo_s = dict(wr=wr, row_blk0=bp * lp // tm_s, nb=1, nt=bs * ls // tm_s, tm=tm_s, time_major=False)
    *seq_s, g_s = rwkv_prep(zm, p, layer, e_mat, **geo_s)
    seq_tm = [t.reshape(bs, ls, wr).transpose(1, 0, 2).reshape(ls, bs * wr // LANES, LANES) for t in seq_s]
    y_s, s_s = wkv7(seq_tm, _wkv_state_in(wkv0_s, heads, n))
    y_s = y_s.reshape(ls, bs, wr).transpose(1, 0, 2).reshape(bs * ls, wr)
    y_s = rwkv_post(y_s, zm, seq_s[3], g_s, p, e_mat, n=n, **geo_s)

    y = jnp.concatenate([y_p, y_s, jnp.zeros((tp - t_real, wr), BF16)], axis=0)
    return (y, _wkv_state_out(s_p, bp, heads, n), _wkv_state_out(s_s, bs, heads, n), zp[:, -1], zs[:, -1])


def kernel(x_prompt, x_sample, state_s5_re, state_s5_im, state_rwkv_shift, state_rwkv_wkv, state_hgrn, meta_tokens, norm_mix_w, w_in, s5_lam_re, s5_lam_im, s5_log_dt, s5_b_re, s5_b_im, s5_c_re, s5_c_im, s5_d, s5_w_glu, s5_b_glu, rwkv_mu, rwkv_w0, rwkv_w_up, rwkv_a0, rwkv_a_up, rwkv_g_up, rwkv_k_k, rwkv_k_a, rwkv_r_k, rwkv_ln_w, rwkv_ln_b, hgrn_lb, hgrn_norm_w, p_s5, p_rwkv, p_hgrn, w_out, norm_ffn_w, router_g_w, router_g_b, router_e_w, router_e_b, moe_w_gate, moe_w_up, moe_w_down, final_norm_w):
    depth, D, _ = w_in.shape
    bp, seq, _ = x_prompt.shape
    bs, ls, _ = x_sample.shape
    n_meta = meta_tokens.shape[0]
    lp = seq + n_meta
    w_s5 = s5_w_glu.shape[1]
    heads_r, n_r = rwkv_r_k.shape[1], rwkv_r_k.shape[2]
    c_rwkv = rwkv_mu.shape[1]
    heads_h, dk_h = state_hgrn.shape[2], state_hgrn.shape[3]
    w_hg = heads_h * dk_h
    t_real = bp * lp + bs * ls
    tile = 1280 if t_real > 4096 else 256
    tp = -(-t_real // tile) * tile

    meta = jnp.broadcast_to(meta_tokens[None].astype(F32), (bp, n_meta, D))
    xp = jnp.concatenate([meta, x_prompt.astype(F32)], axis=1).reshape(bp * lp, D)
    h = jnp.concatenate([xp, x_sample.astype(F32).reshape(bs * ls, D), jnp.zeros((tp - t_real, D), F32)], axis=0)

    sm = jax.nn.softmax(hgrn_lb.astype(F32), axis=0)
    lower = jnp.maximum(jnp.cumsum(sm, axis=0) - sm[0], 0.0)

    outs = {k: [] for k in ('p_re', 'p_im', 'p_shift', 'p_wkv', 'p_hg', 's_re', 's_im', 's_shift', 's_wkv', 's_hg')}
    dims = dict(bp=bp, lp=lp, bs=bs, ls=ls)
    for l in range(depth):
        (n16,) = rms_norm(h, norm_mix_w[l].astype(F32), want32=False, want16=True)
        z = mm(n16, w_in, l)
        s5p = _s5_params(s5_lam_re[l], s5_lam_im[l], s5_log_dt[l], s5_b_re[l], s5_b_im[l],
                         s5_c_re[l], s5_c_im[l], s5_d[l])
        y_s5, pre, pim, sre, sim = _s5_branch(z[:, :w_s5], s5p, state_s5_re[l], state_s5_im[l],
                                              s5_w_glu, s5_b_glu[l], l, tp=tp, **dims)
        rp = dict(mu=rwkv_mu[l], w0=rwkv_w0[l], w_up=rwkv_w_up, a0=rwkv_a0[l], a_up=rwkv_a_up,
                  g_up=rwkv_g_up, k_k=rwkv_k_k[l], k_a=rwkv_k_a[l], r_k=rwkv_r_k[l],
                  ln_w=rwkv_ln_w[l], ln_b=rwkv_ln_b[l])
        y_rw, wkv_p, wkv_s, shift_p, shift_s = _rwkv_branch(
            z[:, w_s5:w_s5 + c_rwkv], state_rwkv_shift[l], state_rwkv_wkv[l], rp, l,
            tp=tp, heads=heads_r, n=n_r, **dims)
        o_p, o_s, hg_p, hg_s = hgrn2(z, w_s5 + c_rwkv, lower[l], hgrn_norm_w[l], state_hgrn[l],
                                     heads=heads_h, dk=dk_h, **dims)
        y_hg = jnp.concatenate([o_p, o_s, jnp.zeros((tp - t_real, w_hg), F32)], axis=0)
        g0 = w_s5 + c_rwkv + 4 * w_hg
        merged = merge_branches(z, g0, y_s5.astype(BF16), y_rw.astype(BF16), y_hg.astype(BF16),
                                p_s5, p_rwkv, p_hgrn, l)
        h = mm(merged, w_out, l, res=h)
        (n32,) = rms_norm(h, norm_ffn_w[l].astype(F32), want32=True, want16=False)
        h = moe_ffn(h, n32, t_real, router_g_w[l], router_g_b[l], router_e_w[l], router_e_b[l],
                    moe_w_gate, moe_w_up, moe_w_down, l)
        for key, val in (('p_re', pre), ('p_im', pim), ('p_shift', shift_p), ('p_wkv', wkv_p), ('p_hg', hg_p),
                         ('s_re', sre), ('s_im', sim), ('s_shift', shift_s), ('s_wkv', wkv_s), ('s_hg', hg_s)):
            outs[key].append(val)

    (y,) = rms_norm(h, final_norm_w.astype(F32), want32=True, want16=False)
    yp = y[:bp * lp].reshape(bp, lp, D)[:, n_meta:]
    ys = y[bp * lp:t_real].reshape(bs, ls, D)
    st = lambda key: jnp.stack(outs[key], axis=0).astype(F32)
    return (yp, ys, st('p_re'), st('p_im'), st('p_shift'), st('p_wkv'), st('p_hg'),
            st('s_re'), st('s_im'), st('s_shift'), st('s_wkv'), st('s_hg'))
```
